```python
import math
import jax, jax.numpy as jnp
from jax import lax
import numpy as np

D_MODEL = 1024
BATCH = 16
SEQ = 2048
DEPTH = 2

SSD_HEADS = 16
SSD_HEAD_DIM = 64
SSD_WIDTH = SSD_HEADS * SSD_HEAD_DIM
SSD_GROUPS = 4
SSD_STATE = 128
CONV_K = 4
SSD_CHUNK = 128
CONV_DIM = SSD_WIDTH + 2 * SSD_GROUPS * SSD_STATE

ATT_HEADS = 16
ATT_HEAD_DIM = 64
ATT_WIDTH = ATT_HEADS * ATT_HEAD_DIM
ATT_BRANCHES = ((128, 1), (512, 4), (2048, 16))
BAND_BLOCK = 128

MIX_WIDTH = SSD_WIDTH + ATT_WIDTH
IN_PROJ = SSD_WIDTH + CONV_DIM + SSD_HEADS + 3 * ATT_WIDTH
D_FF = 2816
EPS = 1e-6

kernel_name = "hymba_ssd_dilated_macaron"


def rms_norm(x, w):
    xf = x.astype(jnp.float32)
    y = xf * lax.rsqrt(jnp.mean(xf * xf, axis=-1, keepdims=True) + EPS)
    return (y * w.astype(jnp.float32)).astype(x.dtype)


def swiglu(x, w_gate, w_up, w_down):
    return (jax.nn.silu(x @ w_gate) * (x @ w_up)) @ w_down


def causal_depthwise_conv(x, w, b):
    c = x.shape[-1]
    y = lax.conv_general_dilated(
        x, w[:, None, :], window_strides=(1,), padding=[(CONV_K - 1, 0)],
        dimension_numbers=("NWC", "WIO", "NWC"), feature_group_count=c)
    return y + b


def ssd_mixer(xbc, dt_raw, z, dt_bias, a_log, d_skip, norm_w):
    out_dtype = z.dtype
    b, s, _ = xbc.shape
    H, P, G, N, L = SSD_HEADS, SSD_HEAD_DIM, SSD_GROUPS, SSD_STATE, SSD_CHUNK
    R = H // G
    nc = s // L
    xbc = xbc.astype(jnp.float32)
    xs, bm, cm = jnp.split(xbc, [SSD_WIDTH, SSD_WIDTH + G * N], axis=-1)
    dt = jax.nn.softplus(dt_raw.astype(jnp.float32) + dt_bias.astype(jnp.float32))
    a = -jnp.exp(a_log.astype(jnp.float32))
    xh = xs.reshape(b, s, H, P)
    X = (xh * dt[..., None]).reshape(b, nc, L, G, R, P)
    adt = (dt * a).reshape(b, nc, L, G, R).transpose(0, 3, 4, 1, 2)
    a_cum = jnp.cumsum(adt, axis=-1)
    Bc = bm.reshape(b, nc, L, G, N)
    Cc = cm.reshape(b, nc, L, G, N)
    causal = jnp.tril(jnp.ones((L, L), dtype=bool))
    seg = a_cum[..., :, None] - a_cum[..., None, :]
    lmat = jnp.where(causal, jnp.exp(jnp.where(causal, seg, 0.0)), 0.0)
    cb = jnp.einsum("bclgn,bcsgn->bgcls", Cc, Bc)
    y_diag = jnp.einsum("bgcls,bgrcls,bcsgrp->bclgrp", cb, lmat, X)
    decay_states = jnp.exp(a_cum[..., -1:] - a_cum)
    states = jnp.einsum("bclgn,bgrcl,bclgrp->bcgrpn", Bc, decay_states, X)
    chunk_decay = jnp.exp(a_cum[..., -1])

    def step(h, inp):
        st, dec = inp
        return h * dec[..., None, None] + st, h

    h0 = jnp.zeros((b, G, R, P, N), jnp.float32)
    _, prev = lax.scan(step, h0, (states.transpose(1, 0, 2, 3, 4, 5),
                                  chunk_decay.transpose(3, 0, 1, 2)))
    y_off = jnp.einsum("bclgn,cbgrpn,bgrcl->bclgrp", Cc, prev, jnp.exp(a_cum))
    y = (y_diag + y_off).reshape(b, s, H, P) + xh * d_skip.astype(jnp.float32)[:, None]
    y = y.reshape(b, s, SSD_WIDTH) * jax.nn.silu(z.astype(jnp.float32))
    yg = y.reshape(b, s, G, SSD_WIDTH // G)
    yg = yg * lax.rsqrt(jnp.mean(yg * yg, axis=-1, keepdims=True) + EPS)
    y = yg.reshape(b, s, SSD_WIDTH) * norm_w.astype(jnp.float32)
    return y.astype(out_dtype)


def dilated_branch(q, k, v, dilation, back):
    b, s, h, e = q.shape
    n = s // dilation
    nb = -(-n // BAND_BLOCK)
    npad = nb * BAND_BLOCK
    blk = BAND_BLOCK

    def blocks(t):
        t = t.reshape(b, n, dilation, h, e)
        t = jnp.pad(t, ((0, 0), (0, npad - n), (0, 0), (0, 0), (0, 0)))
        return t.reshape(b, nb, blk, dilation, h, e)

    def with_prev(t):
        prev = jnp.pad(t[:, :-1], ((0, 0), (1, 0), (0, 0), (0, 0), (0, 0), (0, 0)))
        return jnp.concatenate([prev, t], axis=2)

    qb = blocks(q)
    kk = with_prev(blocks(k))
    vv = with_prev(blocks(v))
    scores = jnp.einsum("bnqrhe,bnkrhe->bnrhqk", qb, kk).astype(jnp.float32)
    scores = scores * (1.0 / math.sqrt(e))
    qi = jnp.arange(blk)[:, None]
    kj = jnp.arange(2 * blk)[None, :]
    dist = qi + blk - kj
    bidx = jnp.arange(nb)[:, None, None]
    valid = (dist >= 0) & (dist <= back) & (bidx * blk - blk + kj >= 0)
    scores = jnp.where(valid[None, :, None, None], scores, -jnp.inf)
    m = jnp.max(scores, axis=-1, keepdims=True)
    p = jnp.exp(scores - m)
    den = jnp.sum(p, axis=-1, keepdims=True)
    o = jnp.einsum("bnrhqk,bnkrhe->bnqrhe", (p / den).astype(v.dtype), vv)
    lse = (m + jnp.log(den))[..., 0]
    o = o.reshape(b, npad, dilation, h, e)[:, :n].reshape(b, s, h, e)
    lse = lse.transpose(0, 1, 4, 2, 3).reshape(b, npad, dilation, h)[:, :n].reshape(b, s, h)
    return o, lse


def dilated_attention(q, k, v):
    outs, lses = [], []
    for window, dilation in ATT_BRANCHES:
        o, lse = dilated_branch(q, k, v, dilation, window // dilation)
        outs.append(o)
        lses.append(lse)
    wts = jax.nn.softmax(jnp.stack(lses, axis=0), axis=0)
    y = jnp.einsum("kbsh,kbshe->bshe", wts.astype(q.dtype), jnp.stack(outs, axis=0))
    return y


def setup_inputs(seed: int = 0) -> dict:
    key = jax.random.key(seed)
    ks = jax.random.split(key, 24)
    f = jnp.float32

    def normal(k, shape, scale):
        return jax.random.normal(k, shape, f) * scale

    def gain(k, shape):
        return 1.0 + 0.02 * jax.random.normal(k, shape, f)

    dt = jnp.exp(jax.random.uniform(ks[10], (DEPTH, SSD_HEADS), f)
                 * (math.log(0.1) - math.log(0.001)) + math.log(0.001))
    return {
        "x": jax.random.normal(ks[0], (BATCH, SEQ, D_MODEL), f),
        "ffn1_norm": gain(ks[1], (DEPTH, D_MODEL)),
        "ffn1_w_gate": normal(ks[2], (DEPTH, D_MODEL, D_FF), D_MODEL ** -0.5),
        "ffn1_w_up": normal(ks[3], (DEPTH, D_MODEL, D_FF), D_MODEL ** -0.5),
        "ffn1_w_down": normal(ks[4], (DEPTH, D_FF, D_MODEL), D_FF ** -0.5),
        "mix_norm": gain(ks[5], (DEPTH, D_MODEL)),
        "w_in": normal(ks[6], (DEPTH, D_MODEL, IN_PROJ), D_MODEL ** -0.5),
        "conv_w": normal(ks[7], (DEPTH, CONV_K, CONV_DIM), CONV_K ** -0.5),
        "conv_b": normal(ks[8], (DEPTH, CONV_DIM), 0.02),
        "dt_bias": dt + jnp.log(-jnp.expm1(-dt)),
        "a_log": jnp.log(jax.random.uniform(ks[11], (DEPTH, SSD_HEADS), f, 1.0, 16.0)),
        "d_skip": gain(ks[12], (DEPTH, SSD_HEADS)),
        "ssd_norm": gain(ks[13], (DEPTH, SSD_WIDTH)),
        "q_norm": gain(ks[14], (DEPTH, ATT_HEAD_DIM)),
        "k_norm": gain(ks[15], (DEPTH, ATT_HEAD_DIM)),
        "w_out": normal(ks[16], (DEPTH, MIX_WIDTH, D_MODEL), MIX_WIDTH ** -0.5),
        "ffn2_norm": gain(ks[17], (DEPTH, D_MODEL)),
        "ffn2_w_gate": normal(ks[18], (DEPTH, D_MODEL, D_FF), D_MODEL ** -0.5),
        "ffn2_w_up": normal(ks[19], (DEPTH, D_MODEL, D_FF), D_MODEL ** -0.5),
        "ffn2_w_down": normal(ks[20], (DEPTH, D_FF, D_MODEL), D_FF ** -0.5),
    }


def reference(x, ffn1_norm, ffn1_w_gate, ffn1_w_up, ffn1_w_down, mix_norm, w_in,
              conv_w, conv_b, dt_bias, a_log, d_skip, ssd_norm, q_norm, k_norm,
              w_out, ffn2_norm, ffn2_w_gate, ffn2_w_up, ffn2_w_down):
    b, s, _ = x.shape
    splits = np.cumsum([SSD_WIDTH, CONV_DIM, SSD_HEADS, ATT_WIDTH, ATT_WIDTH]).tolist()
    for i in range(DEPTH):
        x = x + 0.5 * swiglu(rms_norm(x, ffn1_norm[i]), ffn1_w_gate[i], ffn1_w_up[i], ffn1_w_down[i])
        h = rms_norm(x, mix_norm[i])
        proj = h @ w_in[i]
        z, xbc, dt_raw, q, k, v = jnp.split(proj, splits, axis=-1)
        xbc = jax.nn.silu(causal_depthwise_conv(xbc, conv_w[i], conv_b[i]))
        y_ssd = ssd_mixer(xbc, dt_raw, z, dt_bias[i], a_log[i], d_skip[i], ssd_norm[i])
        q = rms_norm(q.reshape(b, s, ATT_HEADS, ATT_HEAD_DIM), q_norm[i])
        k = rms_norm(k.reshape(b, s, ATT_HEADS, ATT_HEAD_DIM), k_norm[i])
        v = v.reshape(b, s, ATT_HEADS, ATT_HEAD_DIM)
        y_att = dilated_attention(q, k, v).reshape(b, s, ATT_WIDTH)
        x = x + jnp.concatenate([y_ssd, y_att], axis=-1) @ w_out[i]
        x = x + 0.5 * swiglu(rms_norm(x, ffn2_norm[i]), ffn2_w_gate[i], ffn2_w_up[i], ffn2_w_down[i])
    return x
```

```python
import functools
import math

import jax
import jax.numpy as jnp
from jax import lax
from jax.experimental import pallas as pl
from jax.experimental.pallas import tpu as pltpu

F32 = jnp.float32
BF16 = jnp.bfloat16

D_MODEL = 1024
SSD_HEADS = 16
SSD_HEAD_DIM = 64
SSD_WIDTH = SSD_HEADS * SSD_HEAD_DIM
SSD_GROUPS = 4
SSD_STATE = 128
CONV_K = 4
SSD_CHUNK = 128
CONV_DIM = SSD_WIDTH + 2 * SSD_GROUPS * SSD_STATE
ATT_HEADS = 16
ATT_HEAD_DIM = 64
ATT_WIDTH = ATT_HEADS * ATT_HEAD_DIM
ATT_BRANCHES = ((128, 1), (512, 4), (2048, 16))
BAND_BLOCK = 128
D_FF = 2816
EPS = 1e-6

LANES = 128
SUBLANES = 8
VMEM_LIMIT_BYTES = 56 * 1024 * 1024

DT_PAD = LANES
HEAD_REP = 3
COL_Z = 0
COL_XBC = COL_Z + SSD_WIDTH
COL_DT = COL_XBC + CONV_DIM
COL_Q = COL_DT + DT_PAD
COL_K = COL_Q + ATT_WIDTH
COL_V = COL_K + ATT_WIDTH
IN_COLS = COL_V + ATT_WIDTH

TM_FFN = 512
TM_PROJ = 512
TM_OUT = 512


def _const_spec(shape):
    nd = len(shape)
    return pl.BlockSpec(shape, lambda *_: (0,) * nd, pipeline_mode=pl.Buffered(1))


def _params(n_axes):
    return pltpu.CompilerParams(dimension_semantics=("arbitrary",) * n_axes,
                                vmem_limit_bytes=VMEM_LIMIT_BYTES)


def _rms(x, gain):
    ms = jnp.mean(x * x, axis=-1, keepdims=True)
    return x * lax.rsqrt(ms + EPS) * gain


def _silu(x):
    return x * (1.0 / (1.0 + jnp.exp(-x)))


def _dot(a, b):
    return jnp.dot(a, b, preferred_element_type=F32)


def _dot_nt(a, b):
    return lax.dot_general(a, b, (((1,), (1,)), ((), ())), preferred_element_type=F32)


def _dot_tn(a, b):
    return lax.dot_general(a, b, (((0,), (0,)), ((), ())), preferred_element_type=F32)


def _split3(v):
    hi = v.astype(BF16)
    r1 = v - hi.astype(F32)
    mid = r1.astype(BF16)
    lo = (r1 - mid.astype(F32)).astype(BF16)
    return hi, mid, lo


def _split3_lanes(v):
    hi, mid, lo = _split3(v)
    lane = lax.broadcasted_iota(jnp.int32, v.shape, 1)
    zero = jnp.zeros_like(hi)
    return jnp.where(lane < SSD_HEADS, hi,
                     jnp.where(lane < 2 * SSD_HEADS, mid,
                               jnp.where(lane < 3 * SSD_HEADS, lo, zero)))


def _ffn_body(x_ref, g_ref, wg_ref, wu_ref, wd_ref, o_ref):
    x = x_ref[...]
    xn = _rms(x, g_ref[...]).astype(BF16)
    gate = _dot(xn, wg_ref[...])
    up = _dot(xn, wu_ref[...])
    h = (_silu(gate) * up).astype(BF16)
    o_ref[...] = x + 0.5 * _dot(h, wd_ref[...])


def _ffn(x, gain, wg, wu, wd):
    t, d = x.shape
    f = wg.shape[1]
    tm = TM_FFN
    row = pl.BlockSpec((tm, d), lambda i: (i, 0))
    return pl.pallas_call(
        _ffn_body, grid=(t // tm,),
        in_specs=[row, _const_spec((1, d)), _const_spec((d, f)), _const_spec((d, f)),
                  _const_spec((f, d))],
        out_specs=row, out_shape=jax.ShapeDtypeStruct((t, d), F32),
        compiler_params=_params(1), name="ffn")(x, gain, wg, wu, wd)


def _head_norm(x, gain, bd2):
    outs = []
    for c in range(ATT_WIDTH // LANES):
        xc = x[:, c * LANES:(c + 1) * LANES]
        sq = xc * xc
        hi = sq.astype(BF16)
        lo = (sq - hi.astype(F32)).astype(BF16)
        ms = _dot(jnp.concatenate([hi, lo], axis=1), bd2)
        outs.append(xc * lax.rsqrt(ms + EPS) * gain)
    return jnp.concatenate(outs, axis=1)


def _inproj_body(x_ref, g_ref, w_ref, qg_ref, kg_ref, bd2_ref,
                 z_ref, xbc_ref, dt_ref, q_ref, k_ref, v_ref):
    xn = _rms(x_ref[...], g_ref[...]).astype(BF16)
    z_ref[...] = _dot(xn, w_ref[:, COL_Z:COL_XBC])
    xbc_ref[...] = _dot(xn, w_ref[:, COL_XBC:COL_DT])
    dt_ref[...] = _dot(xn, w_ref[:, COL_DT:COL_Q])
    bd2 = bd2_ref[...]
    q = _dot(xn, w_ref[:, COL_Q:COL_K])
    q_ref[...] = _head_norm(q, qg_ref[...], bd2).astype(BF16)
    k = _dot(xn, w_ref[:, COL_K:COL_V])
    k_ref[...] = _head_norm(k, kg_ref[...], bd2).astype(BF16)
    v_ref[...] = _dot(xn, w_ref[:, COL_V:IN_COLS]).astype(BF16)


def _inproj(x, gain, w, qg, kg, bd2):
    t, d = x.shape
    tm = TM_PROJ
    row = lambda n: pl.BlockSpec((tm, n), lambda i: (i, 0))
    shp = lambda n, dt: jax.ShapeDtypeStruct((t, n), dt)
    return pl.pallas_call(
        _inproj_body, grid=(t // tm,),
        in_specs=[row(d), _const_spec((1, d)), _const_spec((d, IN_COLS)),
                  _const_spec((1, LANES)), _const_spec((1, LANES)),
                  _const_spec((2 * LANES, LANES))],
        out_specs=[row(SSD_WIDTH), row(CONV_DIM), row(DT_PAD), row(ATT_WIDTH), row(ATT_WIDTH),
                   row(ATT_WIDTH)],
        out_shape=[shp(SSD_WIDTH, F32), shp(CONV_DIM, F32), shp(DT_PAD, F32),
                   shp(ATT_WIDTH, BF16), shp(ATT_WIDTH, BF16), shp(ATT_WIDTH, BF16)],
        compiler_params=_params(1), name="inproj")(x, gain, w, qg, kg, bd2)


def _ssd_body(xbc_ref, dt_ref, z_ref, cw_ref, cb_ref, dtb_ref, alog_ref, dskip_ref, nw_ref,
              tril_ref, ehead_ref, ecol_ref, y_ref, cbuf_ref, state_ref, acumt_ref):
    L = SSD_CHUNK
    c = pl.program_id(1)

    @pl.when(c == 0)
    def _():
        cbuf_ref[0:SUBLANES, :] = jnp.zeros((SUBLANES, CONV_DIM), F32)
        state_ref[...] = jnp.zeros_like(state_ref)

    cbuf_ref[SUBLANES:SUBLANES + L, :] = xbc_ref[...]
    conv = cb_ref[...]
    for k in range(CONV_K):
        off = SUBLANES - (CONV_K - 1) + k
        conv = conv + cw_ref[k:k + 1, :] * cbuf_ref[off:off + L, :]
    cbuf_ref[0:SUBLANES, :] = cbuf_ref[L:L + SUBLANES, :]
    xbc = _silu(conv)
    xs = xbc[:, :SSD_WIDTH]
    bm = xbc[:, SSD_WIDTH:SSD_WIDTH + SSD_GROUPS * SSD_STATE].astype(BF16)
    cm = xbc[:, SSD_WIDTH + SSD_GROUPS * SSD_STATE:].astype(BF16)

    dt = jax.nn.softplus(dt_ref[...] + dtb_ref[...])
    adt = dt * (-jnp.exp(alog_ref[...]))
    tril = tril_ref[...]
    hi, mid, lo = _split3(adt)
    acum = _dot(tril, hi) + _dot(tril, mid) + _dot(tril, lo)
    acumt_ref[...] = acum.T
    dt_x = _dot(_split3_lanes(dt), ehead_ref[...])
    acum3 = _split3_lanes(acum)
    acum_x = _dot(acum3, ehead_ref[...])
    acum_col = _dot(acum3, ecol_ref[...])

    xdt = xs * dt_x
    eacum_x = jnp.exp(acum_x)
    tot_x = acum_x[L - 1:L, :]
    xdt_b = xdt.astype(BF16)
    xdec_b = (xdt * jnp.exp(tot_x - acum_x)).astype(BF16)
    chunk_decay = eacum_x[L - 1:L, :]

    row = lax.broadcasted_iota(jnp.int32, (L, L), 0)
    col = lax.broadcasted_iota(jnp.int32, (L, L), 1)
    causal = row >= col
    lane = lax.broadcasted_iota(jnp.int32, (L, LANES), 1)
    first_head = lane < SSD_HEAD_DIM
    heads_per_group = SSD_HEADS // SSD_GROUPS
    gw = heads_per_group * SSD_HEAD_DIM

    y_parts = []
    for g in range(SSD_GROUPS):
        cg = cm[:, g * SSD_STATE:(g + 1) * SSD_STATE]
        bg = bm[:, g * SSD_STATE:(g + 1) * SSD_STATE]
        cbm = _dot_nt(cg, bg)
        yd = []
        for r in range(heads_per_group):
            h = g * heads_per_group + r
            seg = acum_col[:, h * L:(h + 1) * L] - acumt_ref[h:h + 1, :]
            m = (cbm * jnp.exp(jnp.where(causal, seg, -jnp.inf))).astype(BF16)
            pair = h // 2
            yd.append(_dot(m, xdt_b[:, pair * LANES:(pair + 1) * LANES]))
        y_diag = jnp.concatenate(
            [jnp.where(first_head, yd[0], yd[1]), jnp.where(first_head, yd[2], yd[3])], axis=1)
        st = state_ref[:, g * gw:(g + 1) * gw]
        y_off = _dot(cg, st.astype(BF16)) * eacum_x[:, g * gw:(g + 1) * gw]
        y_parts.append(y_diag + y_off)
        new_states = _dot_tn(bg, xdec_b[:, g * gw:(g + 1) * gw])
        state_ref[:, g * gw:(g + 1) * gw] = st * chunk_decay[:, g * gw:(g + 1) * gw] + new_states

    y = jnp.concatenate(y_parts, axis=1) + xs * dskip_ref[...]
    y = y * _silu(z_ref[...])
    outs = []
    for g in range(SSD_GROUPS):
        yg = y[:, g * gw:(g + 1) * gw]
        ms = jnp.mean(yg * yg, axis=-1, keepdims=True)
        outs.append(yg * lax.rsqrt(ms + EPS))
    y_ref[...] = (jnp.concatenate(outs, axis=1) * nw_ref[...]).astype(BF16)


def _ssd(xbc, dt, z, cw, cb, dtb, alog, dskip_x, nw, tril, ehead, ecol, batch, seq):
    L = SSD_CHUNK
    nc = seq // L
    blk = lambda n: pl.BlockSpec((L, n), lambda b, c: (b * nc + c, 0))
    return pl.pallas_call(
        _ssd_body, grid=(batch, nc),
        in_specs=[blk(CONV_DIM), blk(DT_PAD), blk(SSD_WIDTH),
                  _const_spec((CONV_K, CONV_DIM)), _const_spec((1, CONV_DIM)),
                  _const_spec((1, DT_PAD)), _const_spec((1, DT_PAD)),
                  _const_spec((1, SSD_WIDTH)), _const_spec((1, SSD_WIDTH)),
                  _const_spec((L, L)), _const_spec((LANES, SSD_WIDTH)),
                  _const_spec((LANES, SSD_HEADS * L))],
        out_specs=blk(SSD_WIDTH),
        out_shape=jax.ShapeDtypeStruct((batch * seq, SSD_WIDTH), BF16),
        scratch_shapes=[pltpu.VMEM((SUBLANES + L, CONV_DIM), F32),
                        pltpu.VMEM((SSD_STATE, SSD_WIDTH), F32),
                        pltpu.VMEM((L, LANES), F32)],
        compiler_params=_params(2), name="ssd")(
            xbc, dt, z, cw, cb, dtb, alog, dskip_x, nw, tril, ehead, ecol)


def _attn_body(q_ref, kc_ref, kp_ref, vc_ref, vp_ref, o_ref, lse_ref, stat_ref, *, has_prev):
    blk = BAND_BLOCK
    ib = pl.program_id(2)
    krow = lax.broadcasted_iota(jnp.int32, (blk, blk), 0)
    qcol = lax.broadcasted_iota(jnp.int32, (blk, blk), 1)
    cur_ok = krow <= qcol
    diag = krow == qcol
    lane = lax.broadcasted_iota(jnp.int32, (blk, LANES), 1)
    first_head = lane < ATT_HEAD_DIM
    srow = lax.broadcasted_iota(jnp.int32, (LANES, blk), 0)
    first_rows = srow < ATT_HEAD_DIM
    neg = jnp.float32(-jnp.inf)
    if has_prev:
        prev_bias = jnp.where(ib > 0, jnp.float32(0.0), neg)

    stat_ref[...] = jnp.zeros_like(stat_ref)
    for pair in range(ATT_HEADS // 2):
        sl = slice(pair * LANES, (pair + 1) * LANES)
        q2 = q_ref[:, sl]
        kc = kc_ref[:, sl]
        vt = vc_ref[:, sl].T
        zero = jnp.zeros_like(kc)
        stack = [jnp.where(first_head, kc, zero), jnp.where(first_head, zero, kc)]
        if has_prev:
            kp = kp_ref[:, sl]
            stack += [jnp.where(first_head, kp, zero), jnp.where(first_head, zero, kp)]
            vt = jnp.concatenate([vt, vp_ref[:, sl].T], axis=1)
        st = _dot_nt(jnp.concatenate(stack, axis=0), q2)
        o_heads = []
        for hh in range(2):
            sc = st[hh * blk:(hh + 1) * blk]
            if has_prev:
                sp = st[(2 + hh) * blk:(3 + hh) * blk] + prev_bias
                comb = jnp.where(cur_ok, sc, sp)
                dg = jnp.max(jnp.where(diag, sp, neg), axis=0, keepdims=True)
                m = jnp.maximum(jnp.max(comb, axis=0, keepdims=True), dg)
                p = jnp.exp(comb - m)
                pd = jnp.exp(dg - m)
                den = jnp.sum(p, axis=0, keepdims=True) + pd
                pc = jnp.where(cur_ok, p, 0.0).astype(BF16)
                pp = jnp.where(cur_ok, 0.0, p)
                pp = jnp.where(diag, pd, pp).astype(BF16)
                pfull = jnp.concatenate([pc, pp], axis=0)
            else:
                comb = jnp.where(cur_ok, sc, neg)
                m = jnp.max(comb, axis=0, keepdims=True)
                p = jnp.exp(comb - m)
                den = jnp.sum(p, axis=0, keepdims=True)
                pfull = p.astype(BF16)
            o_heads.append(_dot(vt, pfull) * (1.0 / den))
            lse = m + jnp.log(den)
            h = 2 * pair + hh
            for rep in range(HEAD_REP):
                stat_ref[rep * ATT_HEADS + h:rep * ATT_HEADS + h + 1, :] = lse
        ot = jnp.where(first_rows, o_heads[0], o_heads[1])
        o_ref[:, sl] = ot.T.astype(BF16)
    lse_ref[...] = stat_ref[...].T


def _attn_branch(q, k, v, batch, seq, dilation):
    blk = BAND_BLOCK
    n = seq // dilation
    nb = n // blk
    has_prev = nb > 1
    w = ATT_WIDTH
    view = lambda a, width: a.reshape(batch, n, dilation * width)
    cur = lambda width: pl.BlockSpec((None, blk, width), lambda b, r, i: (b, i, r))
    prev = lambda width: pl.BlockSpec((None, blk, width),
                                      lambda b, r, i: (b, jnp.maximum(i - 1, 0), r))
    qv, kv, vv = view(q, w), view(k, w), view(v, w)
    o, lse = pl.pallas_call(
        functools.partial(_attn_body, has_prev=has_prev), grid=(batch, dilation, nb),
        in_specs=[cur(w), cur(w), prev(w), cur(w), prev(w)],
        out_specs=[cur(w), cur(LANES)],
        out_shape=[jax.ShapeDtypeStruct((batch, n, dilation * w), BF16),
                   jax.ShapeDtypeStruct((batch, n, dilation * LANES), F32)],
        scratch_shapes=[pltpu.VMEM((LANES, blk), F32)],
        compiler_params=_params(3), name=f"attn_d{dilation}")(qv, kv, kv, vv, vv)
    return o.reshape(batch * seq, w), lse.reshape(batch * seq, LANES)


def _outproj_body(x_ref, ys_ref, o1_ref, o2_ref, o3_ref, l1_ref, l2_ref, l3_ref, ehead_ref,
                  w_ref, out_ref):
    l1, l2, l3 = l1_ref[...], l2_ref[...], l3_ref[...]
    m = jnp.maximum(jnp.maximum(l1, l2), l3)
    e1, e2, e3 = jnp.exp(l1 - m), jnp.exp(l2 - m), jnp.exp(l3 - m)
    inv = 1.0 / (e1 + e2 + e3)
    ehead = ehead_ref[...]
    y_att = (_dot(_split3_lanes(e1 * inv), ehead) * o1_ref[...].astype(F32)
             + _dot(_split3_lanes(e2 * inv), ehead) * o2_ref[...].astype(F32)
             + _dot(_split3_lanes(e3 * inv), ehead) * o3_ref[...].astype(F32))
    out_ref[...] = (x_ref[...] + _dot(ys_ref[...], w_ref[0:SSD_WIDTH, :])
                    + _dot(y_att.astype(BF16), w_ref[SSD_WIDTH:SSD_WIDTH + ATT_WIDTH, :]))


def _outproj(x, y_ssd, outs, lses, ehead, w):
    t, d = x.shape
    tm = TM_OUT
    row = lambda n: pl.BlockSpec((tm, n), lambda i: (i, 0))
    return pl.pallas_call(
        _outproj_body, grid=(t // tm,),
        in_specs=[row(d), row(SSD_WIDTH)] + [row(ATT_WIDTH)] * 3 + [row(LANES)] * 3
        + [_const_spec((LANES, ATT_WIDTH)), _const_spec((SSD_WIDTH + ATT_WIDTH, d))],
        out_specs=row(d), out_shape=jax.ShapeDtypeStruct((t, d), F32),
        compiler_params=_params(1), name="outproj")(x, y_ssd, *outs, *lses, ehead, w)


def _head_expand(width_per_head):
    r = jnp.arange(LANES)[:, None]
    c = jnp.arange(SSD_HEADS * width_per_head)[None, :]
    return ((r < HEAD_REP * SSD_HEADS) & (r % SSD_HEADS == c // width_per_head)).astype(BF16)


def _rep_heads(v):
    return jnp.pad(jnp.tile(v.astype(F32), HEAD_REP), (0, LANES - HEAD_REP * SSD_HEADS))[None, :]


def kernel(x, ffn1_norm, ffn1_w_gate, ffn1_w_up, ffn1_w_down, mix_norm, w_in, conv_w, conv_b,
           dt_bias, a_log, d_skip, ssd_norm, q_norm, k_norm, w_out, ffn2_norm, ffn2_w_gate,
           ffn2_w_up, ffn2_w_down):
    batch, seq, d = x.shape
    depth = w_in.shape[0]
    assert d == D_MODEL and seq % (BAND_BLOCK * max(dl for _, dl in ATT_BRANCHES)) == 0
    assert all(wd // dl == BAND_BLOCK for wd, dl in ATT_BRANCHES)
    t = batch * seq
    assert t % TM_FFN == 0 and t % TM_PROJ == 0 and t % TM_OUT == 0

    tril = jnp.tril(jnp.ones((SSD_CHUNK, SSD_CHUNK), BF16))
    ehead = _head_expand(SSD_HEAD_DIM)
    ecol = _head_expand(SSD_CHUNK)
    half = jnp.arange(LANES) // ATT_HEAD_DIM
    bd = (half[:, None] == half[None, :]).astype(F32) / ATT_HEAD_DIM
    bd2 = jnp.concatenate([bd, bd], axis=0).astype(BF16)
    scale = 1.0 / math.sqrt(ATT_HEAD_DIM)

    xf = x.reshape(t, d)
    for i in range(depth):
        dt_cols = jnp.pad(jnp.tile(w_in[i][:, COL_DT:COL_DT + SSD_HEADS], (1, HEAD_REP)),
                          ((0, 0), (0, DT_PAD - HEAD_REP * SSD_HEADS)))
        w_in_p = jnp.concatenate(
            [w_in[i][:, :COL_DT], dt_cols, w_in[i][:, COL_DT + SSD_HEADS:]], axis=1).astype(BF16)
        qg = jnp.tile(q_norm[i].astype(F32) * scale, LANES // ATT_HEAD_DIM)[None, :]
        kg = jnp.tile(k_norm[i].astype(F32), LANES // ATT_HEAD_DIM)[None, :]

        xf = _ffn(xf, ffn1_norm[i][None, :], ffn1_w_gate[i].astype(BF16),
                  ffn1_w_up[i].astype(BF16), ffn1_w_down[i].astype(BF16))
        z, xbc, dt, q, k, v = _inproj(xf, mix_norm[i][None, :], w_in_p, qg, kg, bd2)
        y_ssd = _ssd(xbc, dt, z, conv_w[i], conv_b[i][None, :], _rep_heads(dt_bias[i]),
                     _rep_heads(a_log[i]), jnp.repeat(d_skip[i].astype(F32), SSD_HEAD_DIM)[None, :],
                     ssd_norm[i][None, :], tril, ehead, ecol, batch, seq)
        outs, lses = [], []
        for _, dilation in ATT_BRANCHES:
            o, lse = _attn_branch(q, k, v, batch, seq, dilation)
            outs.append(o)
            lses.append(lse)
        xf = _outproj(xf, y_ssd, outs, lses, ehead, w_out[i].astype(BF16))
        xf = _ffn(xf, ffn2_norm[i][None, :], ffn2_w_gate[i].astype(BF16),
                  ffn2_w_up[i].astype(BF16), ffn2_w_down[i].astype(BF16))
    return xf.reshape(batch, seq, d)
```

```python
import functools
import math

import numpy as np
import jax
import jax.numpy as jnp
from jax import lax
from jax.experimental import pallas as pl
from jax.experimental.pallas import tpu as pltpu

F32 = jnp.float32
BF16 = jnp.bfloat16

D_MODEL = 1024
SSD_HEADS = 16
SSD_HEAD_DIM = 64
SSD_WIDTH = SSD_HEADS * SSD_HEAD_DIM
SSD_GROUPS = 4
SSD_STATE = 128
CONV_K = 4
SSD_CHUNK = 128
BC_WIDTH = 2 * SSD_GROUPS * SSD_STATE
CONV_DIM = SSD_WIDTH + BC_WIDTH
ATT_HEADS = 16
ATT_HEAD_DIM = 64
ATT_WIDTH = ATT_HEADS * ATT_HEAD_DIM
ATT_BRANCHES = ((128, 1), (512, 4), (2048, 16))
BAND_BLOCK = 128
EPS = 1e-6

LANES = 128
SUBLANES = 8
MXU_COLS = 256
VMEM_LIMIT_BYTES = 56 * 1024 * 1024

LOG2E = math.log2(math.e)
LN2 = math.log(2.0)
MASKED = -1e30
DEN_ROWS = 16
SCORE_LOOKAHEAD = 6
DENSE_LOOKAHEAD = 2

HEAD_REP = 3
RES = 16
N_PAIRS = ATT_HEADS // 2

COL_Z = 0
COL_XBC = COL_Z + SSD_WIDTH
COL_DT = COL_XBC + CONV_DIM
SSD_COLS = COL_DT + LANES

TM = 512
ATT_ROWS = 512
SSD_ROWS = 1024


def _const_spec(shape):
    nd = len(shape)
    return pl.BlockSpec(shape, lambda *_: (0,) * nd, pipeline_mode=pl.Buffered(1))


def _params(n_axes):
    return pltpu.CompilerParams(dimension_semantics=("arbitrary",) * n_axes,
                                vmem_limit_bytes=VMEM_LIMIT_BYTES)


def _rms(x, gain):
    ms = jnp.mean(x * x, axis=-1, keepdims=True)
    return x * lax.rsqrt(ms + EPS) * gain


def _silu(x):
    h = 0.5 * x
    return h + h * jnp.tanh(h)


def _dot(a, b):
    return jnp.dot(a, b, preferred_element_type=F32)


def _dot_nt(a, b):
    return lax.dot_general(a, b, (((1,), (1,)), ((), ())), preferred_element_type=F32)


def _dot_tn(a, b):
    return lax.dot_general(a, b, (((0,), (0,)), ((), ())), preferred_element_type=F32)


def _split3(v):
    hi = v.astype(BF16)
    r1 = v - hi.astype(F32)
    mid = r1.astype(BF16)
    lo = (r1 - mid.astype(F32)).astype(BF16)
    return hi, mid, lo


def _split3_lanes(v):
    hi, mid, lo = _split3(v)
    lane = lax.broadcasted_iota(jnp.int32, v.shape, 1)
    zero = jnp.zeros_like(hi)
    return jnp.where(lane < SSD_HEADS, hi,
                     jnp.where(lane < 2 * SSD_HEADS, mid,
                               jnp.where(lane < 3 * SSD_HEADS, lo, zero)))


def _ffn_body(x_ref, g_ref, wg_ref, wu_ref, wd_ref, o_ref):
    x = x_ref[...]
    xn = _rms(x, g_ref[...]).astype(BF16)
    gate = _dot(xn, wg_ref[...])
    up = _dot(xn, wu_ref[...])
    h = (_silu(gate) * up).astype(BF16)
    o_ref[...] = x + 0.5 * _dot(h, wd_ref[...])


def _ffn(x, gain, wg, wu, wd):
    t, d = x.shape
    f = wg.shape[1]
    row = pl.BlockSpec((TM, d), lambda i: (i, 0))
    return pl.pallas_call(
        _ffn_body, grid=(t // TM,),
        in_specs=[row, _const_spec((1, d)), _const_spec((d, f)), _const_spec((d, f)),
                  _const_spec((f, d))],
        out_specs=row, out_shape=jax.ShapeDtypeStruct((t, d), F32),
        compiler_params=_params(1), name="ffn")(x, gain, wg, wu, wd)


def _proj_ssd_body(x_ref, g_ref, w_ref, cw_ref, cb_ref, z_ref, xs_ref, bc_ref, dt_ref, cbuf_ref,
                   *, tiles_per_seq):
    tm = x_ref.shape[0]
    n_chunks = CONV_DIM // MXU_COLS
    per = MXU_COLS // LANES
    xn = _rms(x_ref[...], g_ref[...]).astype(BF16)

    @pl.when(pl.program_id(0) % tiles_per_seq == 0)
    def _():
        cbuf_ref[:, 0:SUBLANES, :] = jnp.zeros((CONV_DIM // LANES, SUBLANES, LANES), F32)

    main = lambda c: _dot(xn, w_ref[:, COL_XBC + c * MXU_COLS:COL_XBC + (c + 1) * MXU_COLS])
    z_ref[...] = _dot(xn, w_ref[:, COL_Z:COL_XBC]).astype(BF16)
    dt_ref[...] = _dot(xn, w_ref[:, COL_DT:SSD_COLS])
    pending = [main(c) for c in range(DENSE_LOOKAHEAD)]
    for c in range(n_chunks):
        res = pending.pop(0)
        if c + DENSE_LOOKAHEAD < n_chunks:
            pending.append(main(c + DENSE_LOOKAHEAD))
        for s in range(per):
            slab = c * per + s
            lanes = slice(slab * LANES, (slab + 1) * LANES)
            cbuf_ref[slab, SUBLANES:SUBLANES + tm, :] = res[:, s * LANES:(s + 1) * LANES]
            conv = cb_ref[:, lanes]
            for k in range(CONV_K):
                off = SUBLANES - (CONV_K - 1) + k
                conv = conv + cw_ref[k:k + 1, lanes] * cbuf_ref[slab, off:off + tm, :]
            cbuf_ref[slab, 0:SUBLANES, :] = cbuf_ref[slab, tm:tm + SUBLANES, :]
            act = _silu(conv)
            if slab * LANES < SSD_WIDTH:
                xs_ref[:, lanes] = act
            else:
                bc_ref[:, slab * LANES - SSD_WIDTH:(slab + 1) * LANES - SSD_WIDTH] = act.astype(BF16)


def _proj_ssd(x, gain, w, cw, cb, seq):
    t, d = x.shape
    row = lambda n: pl.BlockSpec((TM, n), lambda i: (i, 0))
    shp = lambda n, dt: jax.ShapeDtypeStruct((t, n), dt)
    return pl.pallas_call(
        functools.partial(_proj_ssd_body, tiles_per_seq=seq // TM), grid=(t // TM,),
        in_specs=[row(d), _const_spec((1, d)), _const_spec((d, SSD_COLS)),
                  _const_spec((CONV_K, CONV_DIM)), _const_spec((1, CONV_DIM))],
        out_specs=[row(SSD_WIDTH), row(SSD_WIDTH), row(BC_WIDTH), row(LANES)],
        out_shape=[shp(SSD_WIDTH, BF16), shp(SSD_WIDTH, F32), shp(BC_WIDTH, BF16),
                   shp(LANES, F32)],
        scratch_shapes=[pltpu.VMEM((CONV_DIM // LANES, SUBLANES + TM, LANES), F32)],
        compiler_params=_params(1), name="proj_ssd")(x, gain, w, cw, cb)


def _proj_att_body(x_ref, g_ref, w_ref, qg_ref, kg_ref, bd2_ref,
                   qn_ref, kn_ref, vn_ref, q16_ref, k16_ref, v16_ref, scr_ref):
    tm = x_ref.shape[0]
    ni = tm // RES
    per = MXU_COLS // LANES
    n_slabs = scr_ref.shape[0]
    xn = _rms(x_ref[...], g_ref[...]).astype(BF16)
    bd2 = bd2_ref[...]
    plan = ((qg_ref, qn_ref, q16_ref), (kg_ref, kn_ref, k16_ref), (None, vn_ref, v16_ref))
    chunks = [(t, c) for t in range(3) for c in range(ATT_WIDTH // MXU_COLS)]
    main = lambda t, c: _dot(
        xn, w_ref[:, t * ATT_WIDTH + c * MXU_COLS:t * ATT_WIDTH + (c + 1) * MXU_COLS])
    pending = [main(*ch) for ch in chunks[:DENSE_LOOKAHEAD]]
    for ci, (t, c) in enumerate(chunks):
        gain_ref, nat_ref, l16_ref = plan[t]
        res = pending.pop(0)
        if ci + DENSE_LOOKAHEAD < len(chunks):
            pending.append(main(*chunks[ci + DENSE_LOOKAHEAD]))
        for s in range(per):
            lanes = slice(c * MXU_COLS + s * LANES, c * MXU_COLS + (s + 1) * LANES)
            y = res[:, s * LANES:(s + 1) * LANES]
            if gain_ref is not None:
                sq = y * y
                hi = sq.astype(BF16)
                lo = (sq - hi.astype(F32)).astype(BF16)
                ms = _dot(jnp.concatenate([hi, lo], axis=1), bd2)
                y = y * lax.rsqrt(ms + EPS) * gain_ref[...]
            nat_ref[:, lanes] = y.astype(BF16)
            slab = (ci * per + s) % n_slabs
            scr_ref[slab] = y
            for r in range(RES):
                l16_ref[r, :, lanes] = scr_ref[slab, pl.ds(r, ni, stride=RES), :].astype(BF16)


def _proj_att(x, gain, w, qg, kg, bd2, batch, seq):
    t, d = x.shape
    tps = seq // TM
    ni = TM // RES
    row = lambda n: pl.BlockSpec((TM, n), lambda i: (i, 0))
    l16 = pl.BlockSpec((None, RES, ni, ATT_WIDTH), lambda i: (i // tps, 0, i % tps, 0))
    nat_shape = jax.ShapeDtypeStruct((t, ATT_WIDTH), BF16)
    l16_shape = jax.ShapeDtypeStruct((batch, RES, seq // RES, ATT_WIDTH), BF16)
    return pl.pallas_call(
        _proj_att_body, grid=(t // TM,),
        in_specs=[row(d), _const_spec((1, d)), _const_spec((d, 3 * ATT_WIDTH)),
                  _const_spec((1, LANES)), _const_spec((1, LANES)),
                  _const_spec((2 * LANES, LANES))],
        out_specs=[row(ATT_WIDTH)] * 3 + [l16] * 3,
        out_shape=[nat_shape] * 3 + [l16_shape] * 3,
        scratch_shapes=[pltpu.VMEM((4, TM, LANES), F32)],
        compiler_params=_params(1), name="proj_att")(x, gain, w, qg, kg, bd2)


def _ssd_body(xs_ref, bc_ref, dt_ref, z_ref, dtb_ref, alog_ref, dskip_ref, nw_ref,
              tril_ref, eye_ref, ehead_ref, ecol_ref, y_ref, state_ref, acumt_ref, *, chunks):
    L = SSD_CHUNK
    gsz = SSD_GROUPS * SSD_STATE

    @pl.when(pl.program_id(1) == 0)
    def _():
        state_ref[...] = jnp.zeros_like(state_ref)

    lane = lax.broadcasted_iota(jnp.int32, (L, SSD_WIDTH), 1)
    even_head = (lane // SSD_HEAD_DIM) % 2 == 0
    heads_per_group = SSD_HEADS // SSD_GROUPS
    gw = heads_per_group * SSD_HEAD_DIM
    groups = range(SSD_GROUPS)

    def front(ci):
        rows = slice(ci * L, (ci + 1) * L)
        cgs = [bc_ref[rows, gsz + g * SSD_STATE:gsz + (g + 1) * SSD_STATE] for g in groups]
        bgs = [bc_ref[rows, g * SSD_STATE:(g + 1) * SSD_STATE] for g in groups]
        cbms = [_dot_nt(cgs[g], bgs[g]) for g in groups]
        dt = jax.nn.softplus(dt_ref[rows, :] + dtb_ref[...])
        adt = dt * (-jnp.exp(alog_ref[...]) * LOG2E)
        tril = tril_ref[...]
        hi, mid, lo = _split3(adt)
        acum = _dot(tril, hi) + _dot(tril, mid) + _dot(tril, lo)
        acumt_ref[ci] = acum.T
        dt_x = _dot(_split3_lanes(dt), ehead_ref[...])
        acum3 = _split3_lanes(acum)
        acum_x = _dot(acum3, ehead_ref[...])
        acum_col = _dot(jnp.concatenate([acum3, eye_ref[...]], axis=1), ecol_ref[...])
        return cgs, bgs, cbms, dt_x, acum_x, acum_col

    def back(ci, cgs, bgs, cbms, dt_x, acum_x, acum_col):
        rows = slice(ci * L, (ci + 1) * L)
        xs = xs_ref[rows, :]
        sts = [state_ref[:, g * gw:(g + 1) * gw] for g in groups]
        y_offs = [_dot(cgs[g], sts[g].astype(BF16)) for g in groups]
        xdt = xs * dt_x
        eacum_x = jnp.exp2(acum_x)
        tot_x = acum_x[L - 1:L, :]
        xdt_b = xdt.astype(BF16)
        zero = jnp.zeros_like(xdt_b)
        xdt_even = jnp.where(even_head, xdt_b, zero)
        xdt_odd = jnp.where(even_head, zero, xdt_b)
        xdec_b = (xdt * jnp.exp2(tot_x - acum_x)).astype(BF16)
        chunk_decay = eacum_x[L - 1:L, :]
        for g in groups:
            gs = slice(g * gw, (g + 1) * gw)
            new_states = _dot_tn(bgs[g], xdec_b[:, gs])
            state_ref[:, gs] = sts[g] * chunk_decay[:, gs] + new_states
        y_parts = []
        for g in groups:
            gs = slice(g * gw, (g + 1) * gw)
            ms = []
            for r in range(heads_per_group):
                h = g * heads_per_group + r
                seg = acum_col[:, h * L:(h + 1) * L] - acumt_ref[ci, h:h + 1, :]
                ms.append((cbms[g] * jnp.exp2(seg)).astype(BF16))
            yd = []
            for pr in range(heads_per_group // 2):
                pair = g * (heads_per_group // 2) + pr
                sl = slice(pair * LANES, (pair + 1) * LANES)
                lhs = jnp.concatenate([ms[2 * pr], ms[2 * pr + 1]], axis=1)
                rhs = jnp.concatenate([xdt_even[:, sl], xdt_odd[:, sl]], axis=0)
                yd.append(_dot(lhs, rhs))
            y_parts.append(jnp.concatenate(yd, axis=1) + y_offs[g] * eacum_x[:, gs])

        y = jnp.concatenate(y_parts, axis=1) + xs * dskip_ref[...]
        y = y * _silu(z_ref[rows, :].astype(F32))
        outs = []
        for g in groups:
            yg = y[:, g * gw:(g + 1) * gw]
            msq = jnp.mean(yg * yg, axis=-1, keepdims=True)
            outs.append(yg * lax.rsqrt(msq + EPS))
        y_ref[rows, :] = (jnp.concatenate(outs, axis=1) * nw_ref[...]).astype(BF16)

    pending = front(0)
    for ci in range(chunks):
        ready = pending
        if ci + 1 < chunks:
            pending = front(ci + 1)
        back(ci, *ready)


def _ssd(xs, bc, dt, z, dtb, alog, dskip_x, nw, tril, eye, ehead, ecol, batch, seq):
    L = SSD_CHUNK
    ns = seq // SSD_ROWS
    chunks = SSD_ROWS // L
    blk = lambda n: pl.BlockSpec((SSD_ROWS, n), lambda b, s: (b * ns + s, 0))
    return pl.pallas_call(
        functools.partial(_ssd_body, chunks=chunks), grid=(batch, ns),
        in_specs=[blk(SSD_WIDTH), blk(BC_WIDTH), blk(LANES), blk(SSD_WIDTH),
                  _const_spec((1, LANES)), _const_spec((1, LANES)),
                  _const_spec((1, SSD_WIDTH)), _const_spec((1, SSD_WIDTH)),
                  _const_spec((L, L)), _const_spec((L, L)), _const_spec((LANES, SSD_WIDTH)),
                  _const_spec((LANES + L, SSD_HEADS * L))],
        out_specs=blk(SSD_WIDTH),
        out_shape=jax.ShapeDtypeStruct((batch * seq, SSD_WIDTH), BF16),
        scratch_shapes=[pltpu.VMEM((SSD_STATE, SSD_WIDTH), F32),
                        pltpu.VMEM((chunks, L, LANES), F32)],
        compiler_params=_params(2), name="ssd")(
            xs, bc, dt, z, dtb, alog, dskip_x, nw, tril, eye, ehead, ecol)


def _attn_units(units, stat_ref):
    blk = BAND_BLOCK
    lane = lax.broadcasted_iota(jnp.int32, (blk, LANES), 1)
    first_head = lane < ATT_HEAD_DIM
    srow = lax.broadcasted_iota(jnp.int32, (LANES, blk), 0)
    first_rows = srow < ATT_HEAD_DIM
    stat_ref[...] = jnp.zeros_like(stat_ref)

    def scores(unit):
        q2 = unit["q"]()
        zero = jnp.zeros_like(q2)
        qst = jnp.concatenate([jnp.where(first_head, q2, zero),
                               jnp.where(first_head, zero, q2)], axis=0)
        return _dot_nt(unit["k"](), qst)

    pending = [scores(u) for u in units[:SCORE_LOOKAHEAD]]
    for idx, unit in enumerate(units):
        st = pending.pop(0)
        if idx + SCORE_LOOKAHEAD < len(units):
            pending.append(scores(units[idx + SCORE_LOOKAHEAD]))
        bias = unit["bias"]()
        ps, mxs = [], []
        for hh in range(2):
            s = st[:, hh * blk:(hh + 1) * blk] + bias
            mx = jnp.max(s, axis=0, keepdims=True)
            ps.append(jnp.exp2(s - mx).astype(BF16))
            mxs.append(mx)
        vt = unit["vt"]()
        ones = jnp.ones((DEN_ROWS, vt.shape[1]), BF16)
        ot = _dot(jnp.concatenate([vt, ones], axis=0), jnp.concatenate(ps, axis=1))
        outs = []
        for hh in range(2):
            den = ot[LANES:LANES + 1, hh * blk:(hh + 1) * blk]
            outs.append(ot[0:LANES, hh * blk:(hh + 1) * blk] * (1.0 / den))
            lse = (mxs[hh] + jnp.log2(den)) * LN2
            for rep in range(HEAD_REP):
                row = rep * ATT_HEADS + 2 * unit["pair"] + hh
                stat_ref[unit["slot"], row:row + 1, :] = lse
        o_t = jnp.where(first_rows, outs[0], outs[1])
        unit["write"](o_t.T.astype(BF16))


def _attn_d1_body(q_ref, kc_ref, kp_ref, vc_ref, vp_ref, bias_ref, o_ref, lse_ref, stat_ref):
    blk = BAND_BLOCK
    n_sub = ATT_ROWS // blk
    has_prev = jnp.where(pl.program_id(1) > 0, 1, 0)
    units = []
    for p in range(N_PAIRS):
        sl = slice(p * LANES, (p + 1) * LANES)
        vts = {}

        def vt_block(m, sl=sl, vts=vts):
            if m not in vts:
                v = vp_ref[:, sl] if m == 0 else vc_ref[(m - 1) * blk:m * blk, sl]
                vts[m] = v.T
            return vts[m]

        for j in range(n_sub):
            def write(o, j=j, sl=sl):
                o_ref[j * blk:(j + 1) * blk, sl] = o

            if j == 0:
                k_of = lambda sl=sl: jnp.concatenate([kp_ref[:, sl], kc_ref[0:blk, sl]], axis=0)
                bias_of = lambda: bias_ref[has_prev]
            else:
                k_of = lambda j=j, sl=sl: kc_ref[(j - 1) * blk:(j + 1) * blk, sl]
                bias_of = lambda: bias_ref[1]
            units.append(dict(
                q=lambda j=j, sl=sl: q_ref[j * blk:(j + 1) * blk, sl], k=k_of,
                vt=lambda j=j, f=vt_block: jnp.concatenate([f(j), f(j + 1)], axis=1),
                bias=bias_of, slot=j, pair=p, write=write))
    _attn_units(units, stat_ref)
    for j in range(n_sub):
        lse_ref[j * blk:(j + 1) * blk, :] = stat_ref[j].T


def _attn_d4_body(q_ref, kc_ref, kp_ref, vc_ref, vp_ref, bias_ref, o_ref, lse_ref, stat_ref):
    blk = BAND_BLOCK
    ni = blk // 4
    has_prev = jnp.where(pl.program_id(1) > 0, 1, 0)
    gather = lambda ref, r, sl: [ref[4 * a + r, :, sl] for a in range(4)]
    units = []
    for p in range(N_PAIRS):
        sl = slice(p * LANES, (p + 1) * LANES)
        for r in range(4):
            def write(o, r=r, sl=sl):
                for a in range(4):
                    o_ref[4 * a + r, :, sl] = o[a * ni:(a + 1) * ni, :]

            units.append(dict(
                q=lambda r=r, sl=sl: jnp.concatenate(gather(q_ref, r, sl), axis=0),
                k=lambda r=r, sl=sl: jnp.concatenate(
                    gather(kp_ref, r, sl) + gather(kc_ref, r, sl), axis=0),
                vt=lambda r=r, sl=sl: jnp.concatenate(
                    [jnp.concatenate(gather(vp_ref, r, sl), axis=0).T,
                     jnp.concatenate(gather(vc_ref, r, sl), axis=0).T], axis=1),
                bias=lambda: bias_ref[has_prev], slot=r, pair=p, write=write))
    _attn_units(units, stat_ref)
    for r in range(4):
        lt = stat_ref[r].T
        for a in range(4):
            lse_ref[4 * a + r] = lt[a * ni:(a + 1) * ni, :]


def _attn_d16_body(q_ref, k_ref, v_ref, bias_ref, o_ref, lse_ref, stat_ref):
    n_sub = q_ref.shape[0]
    units = []
    for p in range(N_PAIRS):
        sl = slice(p * LANES, (p + 1) * LANES)
        for j in range(n_sub):
            def write(o, j=j, sl=sl):
                o_ref[j, :, sl] = o

            units.append(dict(
                q=lambda j=j, sl=sl: q_ref[j, :, sl], k=lambda j=j, sl=sl: k_ref[j, :, sl],
                vt=lambda j=j, sl=sl: v_ref[j, :, sl].T,
                bias=lambda: bias_ref[0], slot=j, pair=p, write=write))
    _attn_units(units, stat_ref)
    for j in range(n_sub):
        lse_ref[j] = stat_ref[j].T


def _stat_scratch(n_sub):
    return [pltpu.VMEM((n_sub, LANES, BAND_BLOCK), F32)]


def _band_bias(kk, qi, need_prev):
    ok = (kk[:, None] >= qi[None, :]) & (kk[:, None] <= qi[None, :] + BAND_BLOCK)
    if need_prev is not None:
        ok = ok & need_prev[:, None]
    return np.where(ok, 0.0, -np.inf).astype(np.float32)


def _attn_d1(q, k, v, batch, seq):
    blk = BAND_BLOCK
    w = ATT_WIDTH
    nt = seq // ATT_ROWS
    per = ATT_ROWS // blk
    kk = np.arange(2 * blk)
    qi = np.arange(blk)
    bias = jnp.asarray(np.stack([_band_bias(kk, qi, kk >= blk), _band_bias(kk, qi, None)]))
    cur = lambda n: pl.BlockSpec((ATT_ROWS, n), lambda b, i: (b * nt + i, 0))
    prev = pl.BlockSpec((blk, w), lambda b, i: (b * nt * per + jnp.maximum(i * per - 1, 0), 0))
    return pl.pallas_call(
        _attn_d1_body, grid=(batch, nt),
        in_specs=[cur(w), cur(w), prev, cur(w), prev, _const_spec(bias.shape)],
        out_specs=[cur(w), cur(LANES)],
        out_shape=[jax.ShapeDtypeStruct((batch * seq, w), BF16),
                   jax.ShapeDtypeStruct((batch * seq, LANES), F32)],
        scratch_shapes=_stat_scratch(per),
        compiler_params=_params(2), name="attn_d1")(q, k, k, v, v, bias)


def _attn_d4(q16, k16, v16, batch, seq):
    blk = BAND_BLOCK
    w = ATT_WIDTH
    ni = blk // 4
    nb = seq // 4 // blk
    pos = np.arange(blk)
    true_i = 4 * (pos % ni) + pos // ni
    kk = np.concatenate([true_i, true_i + blk])
    bias = jnp.asarray(np.stack([_band_bias(kk, true_i, kk >= blk), _band_bias(kk, true_i, None)]))
    cur = lambda n: pl.BlockSpec((None, RES, ni, n), lambda b, i: (b, 0, i, 0))
    prev = pl.BlockSpec((None, RES, ni, w), lambda b, i: (b, 0, jnp.maximum(i - 1, 0), 0))
    return pl.pallas_call(
        _attn_d4_body, grid=(batch, nb),
        in_specs=[cur(w), cur(w), prev, cur(w), prev, _const_spec(bias.shape)],
        out_specs=[cur(w), cur(LANES)],
        out_shape=[jax.ShapeDtypeStruct((batch, RES, seq // RES, w), BF16),
                   jax.ShapeDtypeStruct((batch, RES, seq // RES, LANES), F32)],
        scratch_shapes=_stat_scratch(4),
        compiler_params=_params(2), name="attn_d4")(q16, k16, k16, v16, v16, bias)


def _attn_d16(q16, k16, v16, batch, seq):
    blk = BAND_BLOCK
    w = ATT_WIDTH
    n_sub = 4
    assert seq // RES == blk
    kk = np.arange(blk)
    bias = jnp.asarray(_band_bias(kk + blk, kk, None)[None])
    spec = lambda n: pl.BlockSpec((None, n_sub, blk, n), lambda b, g: (b, g, 0, 0))
    return pl.pallas_call(
        _attn_d16_body, grid=(batch, RES // n_sub),
        in_specs=[spec(w), spec(w), spec(w), _const_spec(bias.shape)],
        out_specs=[spec(w), spec(LANES)],
        out_shape=[jax.ShapeDtypeStruct((batch, RES, blk, w), BF16),
                   jax.ShapeDtypeStruct((batch, RES, blk, LANES), F32)],
        scratch_shapes=_stat_scratch(n_sub),
        compiler_params=_params(2), name="attn_d16")(q16, k16, v16, bias)


def _outproj_body(x_ref, ys_ref, o1_ref, l1_ref, o4_ref, l4_ref, o16_ref, l16_ref, ehead_ref,
                  w_ref, out_ref, small_ref, perm_ref):
    tm = x_ref.shape[0]
    ni = tm // RES

    def to_natural(l_ref):
        for r in range(RES):
            small_ref[pl.ds(r, ni, stride=RES), :] = l_ref[r]
        return small_ref[...]

    def to_l16(val):
        small_ref[...] = val
        return jnp.concatenate([small_ref[pl.ds(r, ni, stride=RES), :] for r in range(RES)], axis=0)

    l1 = l1_ref[...]
    l4 = to_natural(l4_ref)
    l16 = to_natural(l16_ref)
    m = jnp.maximum(jnp.maximum(l1, l4), l16)
    e1, e4, e16 = jnp.exp(l1 - m), jnp.exp(l4 - m), jnp.exp(l16 - m)
    inv = 1.0 / (e1 + e4 + e16)
    ehead = ehead_ref[...]
    w1_x = _dot(_split3_lanes(e1 * inv), ehead)
    w4_x = _dot(_split3_lanes(to_l16(e4 * inv)), ehead)
    w16_x = _dot(_split3_lanes(to_l16(e16 * inv)), ehead)
    acc = x_ref[...] + _dot(ys_ref[...], w_ref[0:SSD_WIDTH, :])
    o4 =o4_ref[...].reshape(tm, ATT_WIDTH).astype(F32)
    o16 = o16_ref[...].reshape(tm, ATT_WIDTH).astype(F32)
    y_l16 = w4_x * o4 + w16_x * o16
    for r in range(RES):
        for c in range(ATT_WIDTH // LANES):
            perm_ref[c, pl.ds(r, ni, stride=RES), :] = y_l16[r * ni:(r + 1) * ni,
                                                             c * LANES:(c + 1) * LANES]
    y_nat = jnp.concatenate([perm_ref[c] for c in range(ATT_WIDTH // LANES)], axis=1)
    y_att = w1_x * o1_ref[...].astype(F32) + y_nat
    out_ref[...] = acc + _dot(y_att.astype(BF16), w_ref[SSD_WIDTH:SSD_WIDTH + ATT_WIDTH, :])


def _outproj(x, y_ssd, o1, l1, o4, l4, o16, l16, ehead, w, seq):
    t, d = x.shape
    tps = seq // TM
    ni = TM // RES
    row = lambda n: pl.BlockSpec((TM, n), lambda i: (i, 0))
    slab = lambda n: pl.BlockSpec((None, RES, ni, n), lambda i: (i // tps, 0, i % tps, 0))
    return pl.pallas_call(
        _outproj_body, grid=(t // TM,),
        in_specs=[row(d), row(SSD_WIDTH), row(ATT_WIDTH), row(LANES), slab(ATT_WIDTH), slab(LANES),
                  slab(ATT_WIDTH), slab(LANES), _const_spec((LANES, ATT_WIDTH)),
                  _const_spec((SSD_WIDTH + ATT_WIDTH, d))],
        out_specs=row(d), out_shape=jax.ShapeDtypeStruct((t, d), F32),
        scratch_shapes=[pltpu.VMEM((TM, LANES), F32),
                        pltpu.VMEM((ATT_WIDTH // LANES, TM, LANES), F32)],
        compiler_params=_params(1), name="outproj")(x, y_ssd, o1, l1, o4, l4, o16, l16, ehead, w)


def _head_expand(width_per_head):
    r = jnp.arange(LANES)[:, None]
    c = jnp.arange(SSD_HEADS * width_per_head)[None, :]
    return ((r < HEAD_REP * SSD_HEADS) & (r % SSD_HEADS == c // width_per_head)).astype(BF16)


def _rep_heads(v):
    return jnp.pad(jnp.tile(v.astype(F32), HEAD_REP), (0, LANES - HEAD_REP * SSD_HEADS))[None, :]


def kernel(x, ffn1_norm, ffn1_w_gate, ffn1_w_up, ffn1_w_down, mix_norm, w_in, conv_w, conv_b,
           dt_bias, a_log, d_skip, ssd_norm, q_norm, k_norm, w_out, ffn2_norm, ffn2_w_gate,
           ffn2_w_up, ffn2_w_down):
    batch, seq, d = x.shape
    depth = w_in.shape[0]
    assert d == D_MODEL and seq == RES * BAND_BLOCK
    assert all(wd // dl == BAND_BLOCK for wd, dl in ATT_BRANCHES)
    assert seq % TM == 0 and seq % ATT_ROWS == 0 and seq % SSD_ROWS == 0
    t = batch * seq

    tril = jnp.tril(jnp.ones((SSD_CHUNK, SSD_CHUNK), BF16))
    eye = jnp.eye(SSD_CHUNK, dtype=BF16)
    ehead = _head_expand(SSD_HEAD_DIM)
    masked = jnp.where(tril > 0, 0.0, MASKED).astype(BF16)
    ecol = jnp.concatenate([_head_expand(SSD_CHUNK), jnp.tile(masked, (1, SSD_HEADS))], axis=0)
    half = jnp.arange(LANES) // ATT_HEAD_DIM
    bd = (half[:, None] == half[None, :]).astype(F32) / ATT_HEAD_DIM
    bd2 = jnp.concatenate([bd, bd], axis=0).astype(BF16)
    scale = LOG2E / math.sqrt(ATT_HEAD_DIM)

    xf = x.reshape(t, d)
    for i in range(depth):
        dt_cols = jnp.pad(jnp.tile(w_in[i][:, COL_DT:COL_DT + SSD_HEADS], (1, HEAD_REP)),
                          ((0, 0), (0, LANES - HEAD_REP * SSD_HEADS)))
        w_ssd = jnp.concatenate([w_in[i][:, :COL_DT], dt_cols], axis=1).astype(BF16)
        w_att = w_in[i][:, COL_DT + SSD_HEADS:].astype(BF16)
        qg = jnp.tile(q_norm[i].astype(F32) * scale, LANES // ATT_HEAD_DIM)[None, :]
        kg = jnp.tile(k_norm[i].astype(F32), LANES // ATT_HEAD_DIM)[None, :]

        xf = _ffn(xf, ffn1_norm[i][None, :], ffn1_w_gate[i].astype(BF16),
                  ffn1_w_up[i].astype(BF16), ffn1_w_down[i].astype(BF16))
        z, xs, bc, dt = _proj_ssd(xf, mix_norm[i][None, :], w_ssd, conv_w[i], conv_b[i][None, :], seq)
        qn, kn, vn, q16, k16, v16 = _proj_att(xf, mix_norm[i][None, :], w_att, qg, kg, bd2,
                                              batch, seq)
        y_ssd = _ssd(xs, bc, dt, z, _rep_heads(dt_bias[i]), _rep_heads(a_log[i]),
                     jnp.repeat(d_skip[i].astype(F32), SSD_HEAD_DIM)[None, :],
                     ssd_norm[i][None, :], tril, eye, ehead, ecol, batch, seq)
        o1, l1 = _attn_d1(qn, kn, vn, batch, seq)
        o4, l4 = _attn_d4(q16, k16, v16, batch, seq)
        o16, l16 = _attn_d16(q16, k16, v16, batch, seq)
        xf = _outproj(xf, y_ssd, o1, l1, o4, l4, o16, l16, ehead, w_out[i].astype(BF16), seq)
        xf = _ffn(xf, ffn2_norm[i][None, :], ffn2_w_gate[i].astype(BF16),
                  ffn2_w_up[i].astype(BF16), ffn2_w_down[i].astype(BF16))
    return xf.reshape(batch, seq, d)
```

```python
import functools
import math

import numpy as np
import jax
import jax.numpy as jnp
from jax import lax
from jax.experimental import pallas as pl
from jax.experimental.pallas import tpu as pltpu

F32 = jnp.float32
BF16 = jnp.bfloat16

D_MODEL = 1024
SSD_HEADS = 16
SSD_HEAD_DIM = 64
SSD_WIDTH = SSD_HEADS * SSD_HEAD_DIM
SSD_GROUPS = 4
SSD_STATE = 128
CONV_K = 4
SSD_CHUNK = 128
BC_WIDTH = 2 * SSD_GROUPS * SSD_STATE
CONV_DIM = SSD_WIDTH + BC_WIDTH
ATT_HEADS = 16
ATT_HEAD_DIM = 64
ATT_WIDTH = ATT_HEADS * ATT_HEAD_DIM
ATT_BRANCHES = ((128, 1), (512, 4), (2048, 16))
BAND_BLOCK = 128
EPS = 1e-6

LANES = 128
SUBLANES = 8
MXU_COLS = 256
VMEM_LIMIT_BYTES = 56 * 1024 * 1024

LOG2E = math.log2(math.e)
LN2 = math.log(2.0)
MASKED = -1e30
DEN_ROWS = 16
SCORE_LOOKAHEAD = 6
DENSE_LOOKAHEAD = 2

HEAD_REP = 3
RES = 16
N_PAIRS = ATT_HEADS // 2

COL_Z = 0
COL_XBC = COL_Z + SSD_WIDTH
COL_DT = COL_XBC + CONV_DIM
SSD_COLS = COL_DT + LANES

TM = 512
ATT_ROWS = 512
SSD_ROWS = 1024


def _const_spec(shape):
    nd = len(shape)
    return pl.BlockSpec(shape, lambda *_: (0,) * nd, pipeline_mode=pl.Buffered(1))


def _params(n_axes):
    return pltpu.CompilerParams(dimension_semantics=("arbitrary",) * n_axes,
                                vmem_limit_bytes=VMEM_LIMIT_BYTES)


def _rms(x, gain):
    ms = jnp.mean(x * x, axis=-1, keepdims=True)
    return x * lax.rsqrt(ms + EPS) * gain


def _silu(x):
    h = 0.5 * x
    return h + h * jnp.tanh(h)


def _dot(a, b):
    return jnp.dot(a, b, preferred_element_type=F32)


def _dot_nt(a, b):
    return lax.dot_general(a, b, (((1,), (1,)), ((), ())), preferred_element_type=F32)


def _dot_tn(a, b):
    return lax.dot_general(a, b, (((0,), (0,)), ((), ())), preferred_element_type=F32)


def _split3(v):
    hi = v.astype(BF16)
    r1 = v - hi.astype(F32)
    mid = r1.astype(BF16)
    lo = (r1 - mid.astype(F32)).astype(BF16)
    return hi, mid, lo


def _split3_lanes(v):
    hi, mid, lo = _split3(v)
    lane = lax.broadcasted_iota(jnp.int32, v.shape, 1)
    zero = jnp.zeros_like(hi)
    return jnp.where(lane < SSD_HEADS, hi,
                     jnp.where(lane < 2 * SSD_HEADS, mid,
                               jnp.where(lane < 3 * SSD_HEADS, lo, zero)))


def _ffn_body(x_ref, g_ref, wg_ref, wu_ref, wd_ref, o_ref):
    x = x_ref[...]
    xn = _rms(x, g_ref[...]).astype(BF16)
    gate = _dot(xn, wg_ref[...])
    up = _dot(xn, wu_ref[...])
    h = (_silu(gate) * up).astype(BF16)
    o_ref[...] = x + 0.5 * _dot(h, wd_ref[...])


def _ffn(x, gain, wg, wu, wd):
    t, d = x.shape
    f = wg.shape[1]
    row = pl.BlockSpec((TM, d), lambda i: (i, 0))
    return pl.pallas_call(
        _ffn_body, grid=(t // TM,),
        in_specs=[row, _const_spec((1, d)), _const_spec((d, f)), _const_spec((d, f)),
                  _const_spec((f, d))],
        out_specs=row, out_shape=jax.ShapeDtypeStruct((t, d), F32),
        compiler_params=_params(1), name="ffn")(x, gain, wg, wu, wd)


def _inproj_body(x_ref, g_ref, watt_ref, wssd_ref, qg_ref, kg_ref, bd2_ref, cw_ref, cb_ref,
                 qn_ref, kn_ref, vn_ref, q16_ref, k16_ref, v16_ref, z_ref, xs_ref, bc_ref, dt_ref,
                 raw_ref, scr_ref, cbuf_ref, *, tiles_per_seq):
    tm = x_ref.shape[0]
    ni = tm // RES
    per = MXU_COLS // LANES
    n_raw = raw_ref.shape[0]
    xn = _rms(x_ref[...], g_ref[...]).astype(BF16)
    bd2 = bd2_ref[...]

    @pl.when(pl.program_id(0) % tiles_per_seq == 0)
    def _():
        cbuf_ref[:, 0:SUBLANES, :] = jnp.zeros((CONV_DIM // LANES, SUBLANES, LANES), F32)

    jobs = []
    plan = ((qg_ref, qn_ref, q16_ref), (kg_ref, kn_ref, k16_ref), (None, vn_ref, v16_ref))
    for t in range(3):
        for c in range(ATT_WIDTH // MXU_COLS):
            col0 = c * MXU_COLS
            slot0 = (len(jobs) * per) % n_raw

            def issue(t=t, col0=col0, slot0=slot0):
                res = _dot(xn, watt_ref[:, t * ATT_WIDTH + col0:t * ATT_WIDTH + col0 + MXU_COLS])
                for s in range(per):
                    raw_ref[slot0 + s] = res[:, s * LANES:(s + 1) * LANES]

            def moments(t=t, slot0=slot0):
                if plan[t][0] is None:
                    return None
                out = []
                for s in range(per):
                    y = raw_ref[slot0 + s]
                    sq = y * y
                    hi = sq.astype(BF16)
                    lo = (sq - hi.astype(F32)).astype(BF16)
                    out.append(_dot(jnp.concatenate([hi, lo], axis=1), bd2))
                return out

            def finish(ms, t=t, col0=col0, slot0=slot0):
                gain_ref, nat_ref, l16_ref = plan[t]
                for s in range(per):
                    lanes = slice(col0 + s * LANES, col0 + (s + 1) * LANES)
                    y = raw_ref[slot0 + s]
                    if gain_ref is not None:
                        y = y * lax.rsqrt(ms[s] + EPS) * gain_ref[...]
                    nat_ref[:, lanes] = y.astype(BF16)
                    scr_ref[slot0 + s] = y
                    for r in range(RES):
                        l16_ref[r, :, lanes] = (
                            scr_ref[slot0 + s, pl.ds(r, ni, stride=RES), :].astype(BF16))

            jobs.append((issue, moments, finish))

    for c in range(SSD_WIDTH // MXU_COLS):
        def issue(c=c):
            cols = slice(COL_Z + c * MXU_COLS, COL_Z + (c + 1) * MXU_COLS)
            z_ref[:, cols] = _dot(xn, wssd_ref[:, cols]).astype(BF16)

        jobs.append((issue, lambda: None, lambda _: None))

    def issue_dt():
        dt_ref[...] = _dot(xn, wssd_ref[:, COL_DT:SSD_COLS])

    jobs.append((issue_dt, lambda: None, lambda _: None))

    for c in range(CONV_DIM // MXU_COLS):
        def issue(c=c):
            res = _dot(xn, wssd_ref[:, COL_XBC + c * MXU_COLS:COL_XBC + (c + 1) * MXU_COLS])
            for s in range(per):
                cbuf_ref[c * per + s, SUBLANES:SUBLANES + tm, :] = res[:, s * LANES:(s + 1) * LANES]

        def finish(_, c=c):
            for s in range(per):
                slab = c * per + s
                lanes = slice(slab * LANES, (slab + 1) * LANES)
                conv = cb_ref[:, lanes]
                for k in range(CONV_K):
                    off = SUBLANES - (CONV_K - 1) + k
                    conv = conv + cw_ref[k:k + 1, lanes] * cbuf_ref[slab, off:off + tm, :]
                cbuf_ref[slab, 0:SUBLANES, :] = cbuf_ref[slab, tm:tm + SUBLANES, :]
                act = _silu(conv)
                if slab * LANES < SSD_WIDTH:
                    xs_ref[:, lanes] = act
                else:
                    bc_ref[:, slab * LANES - SSD_WIDTH:(slab + 1) * LANES - SSD_WIDTH] = (
                        act.astype(BF16))

        jobs.append((issue, lambda: None, finish))

    for job in jobs[:DENSE_LOOKAHEAD]:
        job[0]()
    ms_next = jobs[0][1]()
    for idx, job in enumerate(jobs):
        if idx + DENSE_LOOKAHEAD < len(jobs):
            jobs[idx + DENSE_LOOKAHEAD][0]()
        ms = ms_next
        if idx + 1 < len(jobs):
            ms_next = jobs[idx + 1][1]()
        job[2](ms)


def _inproj(x, gain, w_att, w_ssd, qg, kg, bd2, cw, cb, batch, seq):
    t, d = x.shape
    tps = seq // TM
    ni = TM // RES
    row = lambda n: pl.BlockSpec((TM, n), lambda i: (i, 0))
    l16 = pl.BlockSpec((None, RES, ni, ATT_WIDTH), lambda i: (i // tps, 0, i % tps, 0))
    shp = lambda n, dt: jax.ShapeDtypeStruct((t, n), dt)
    l16_shape = jax.ShapeDtypeStruct((batch, RES, seq // RES, ATT_WIDTH), BF16)
    n_raw = (DENSE_LOOKAHEAD + 2) * (MXU_COLS // LANES)
    return pl.pallas_call(
        functools.partial(_inproj_body, tiles_per_seq=tps), grid=(t // TM,),
        in_specs=[row(d), _const_spec((1, d)), _const_spec((d, 3 * ATT_WIDTH)),
                  _const_spec((d, SSD_COLS)), _const_spec((1, LANES)), _const_spec((1, LANES)),
                  _const_spec((2 * LANES, LANES)), _const_spec((CONV_K, CONV_DIM)),
                  _const_spec((1, CONV_DIM))],
        out_specs=[row(ATT_WIDTH)] * 3 + [l16] * 3
        + [row(SSD_WIDTH), row(SSD_WIDTH), row(BC_WIDTH), row(LANES)],
        out_shape=[shp(ATT_WIDTH, BF16)] * 3 + [l16_shape] * 3
        + [shp(SSD_WIDTH, BF16), shp(SSD_WIDTH, F32), shp(BC_WIDTH, BF16), shp(LANES, F32)],
        scratch_shapes=[pltpu.VMEM((n_raw, TM, LANES), F32), pltpu.VMEM((n_raw, TM, LANES), F32),
                        pltpu.VMEM((CONV_DIM // LANES, SUBLANES + TM, LANES), F32)],
        compiler_params=_params(1), name="inproj")(x, gain, w_att, w_ssd, qg, kg, bd2, cw, cb)


def _ssd_body(xs_ref, bc_ref, dt_ref, z_ref, dtb_ref, alog_ref, dskip_ref, nw_ref,
              tril_ref, eye_ref, ehead_ref, ecol_ref, y_ref, state_ref, acumt_ref, *, chunks):
    L = SSD_CHUNK
    gsz = SSD_GROUPS * SSD_STATE

    @pl.when(pl.program_id(1) == 0)
    def _():
        state_ref[...] = jnp.zeros_like(state_ref)

    lane = lax.broadcasted_iota(jnp.int32, (L, SSD_WIDTH), 1)
    even_head = (lane // SSD_HEAD_DIM) % 2 == 0
    heads_per_group = SSD_HEADS // SSD_GROUPS
    gw = heads_per_group * SSD_HEAD_DIM
    groups = range(SSD_GROUPS)

    def front(ci):
        rows = slice(ci * L, (ci + 1) * L)
        cgs = [bc_ref[rows, gsz + g * SSD_STATE:gsz + (g + 1) * SSD_STATE] for g in groups]
        bgs = [bc_ref[rows, g * SSD_STATE:(g + 1) * SSD_STATE] for g in groups]
        cbms = [_dot_nt(cgs[g], bgs[g]) for g in groups]
        dt = jax.nn.softplus(dt_ref[rows, :] + dtb_ref[...])
        adt = dt * (-jnp.exp(alog_ref[...]) * LOG2E)
        tril = tril_ref[...]
        hi, mid, lo = _split3(adt)
        acum = _dot(tril, hi) + _dot(tril, mid) + _dot(tril, lo)
        acumt_ref[ci] = acum.T
        dt_x = _dot(_split3_lanes(dt), ehead_ref[...])
        acum3 = _split3_lanes(acum)
        acum_x = _dot(acum3, ehead_ref[...])
        acum_col = _dot(jnp.concatenate([acum3, eye_ref[...]], axis=1), ecol_ref[...])
        return cgs, bgs, cbms, dt_x, acum_x, acum_col

    def back(ci, cgs, bgs, cbms, dt_x, acum_x, acum_col):
        rows = slice(ci * L, (ci + 1) * L)
        xs = xs_ref[rows, :]
        sts = [state_ref[:, g * gw:(g + 1) * gw] for g in groups]
        y_offs = [_dot(cgs[g], sts[g].astype(BF16)) for g in groups]
        xdt = xs * dt_x
        eacum_x = jnp.exp2(acum_x)
        tot_x = acum_x[L - 1:L, :]
        xdt_b = xdt.astype(BF16)
        zero = jnp.zeros_like(xdt_b)
        xdt_even = jnp.where(even_head, xdt_b, zero)
        xdt_odd = jnp.where(even_head, zero, xdt_b)
        xdec_b = (xdt * jnp.exp2(tot_x - acum_x)).astype(BF16)
        chunk_decay = eacum_x[L - 1:L, :]
        for g in groups:
            gs = slice(g * gw, (g + 1) * gw)
            new_states = _dot_tn(bgs[g], xdec_b[:, gs])
            state_ref[:, gs] = sts[g] * chunk_decay[:, gs] + new_states
        y_parts = []
        for g in groups:
            gs = slice(g * gw, (g + 1) * gw)
            ms = []
            for r in range(heads_per_group):
                h = g * heads_per_group + r
                seg = acum_col[:, h * L:(h + 1) * L] - acumt_ref[ci, h:h + 1, :]
                ms.append((cbms[g] * jnp.exp2(seg)).astype(BF16))
            yd = []
            for pr in range(heads_per_group // 2):
                pair = g * (heads_per_group // 2) + pr
                sl = slice(pair * LANES, (pair + 1) * LANES)
                lhs = jnp.concatenate([ms[2 * pr], ms[2 * pr + 1]], axis=1)
                rhs = jnp.concatenate([xdt_even[:, sl], xdt_odd[:, sl]], axis=0)
                yd.append(_dot(lhs, rhs))
            y_parts.append(jnp.concatenate(yd, axis=1) + y_offs[g] * eacum_x[:, gs])

        y = jnp.concatenate(y_parts, axis=1) + xs * dskip_ref[...]
        y = y * _silu(z_ref[rows, :].astype(F32))
        outs = []
        for g in groups:
            yg = y[:, g * gw:(g + 1) * gw]
            msq = jnp.mean(yg * yg, axis=-1, keepdims=True)
            outs.append(yg * lax.rsqrt(msq + EPS))
        y_ref[rows, :] = (jnp.concatenate(outs, axis=1) * nw_ref[...]).astype(BF16)

    pending = front(0)
    for ci in range(chunks):
        ready = pending
        if ci + 1 < chunks:
            pending = front(ci + 1)
        back(ci, *ready)


def _ssd(xs, bc, dt, z, dtb, alog, dskip_x, nw, tril, eye, ehead, ecol, batch, seq):
    L = SSD_CHUNK
    ns = seq // SSD_ROWS
    chunks = SSD_ROWS // L
    blk = lambda n: pl.BlockSpec((SSD_ROWS, n), lambda b, s: (b * ns + s, 0))
    return pl.pallas_call(
        functools.partial(_ssd_body, chunks=chunks), grid=(batch, ns),
        in_specs=[blk(SSD_WIDTH), blk(BC_WIDTH), blk(LANES), blk(SSD_WIDTH),
                  _const_spec((1, LANES)), _const_spec((1, LANES)),
                  _const_spec((1, SSD_WIDTH)), _const_spec((1, SSD_WIDTH)),
                  _const_spec((L, L)), _const_spec((L, L)), _const_spec((LANES, SSD_WIDTH)),
                  _const_spec((LANES + L, SSD_HEADS * L))],
        out_specs=blk(SSD_WIDTH),
        out_shape=jax.ShapeDtypeStruct((batch * seq, SSD_WIDTH), BF16),
        scratch_shapes=[pltpu.VMEM((SSD_STATE, SSD_WIDTH), F32),
                        pltpu.VMEM((chunks, L, LANES), F32)],
        compiler_params=_params(2), name="ssd")(
            xs, bc, dt, z, dtb, alog, dskip_x, nw, tril, eye, ehead, ecol)


def _attn_units(units, stat_ref):
    blk = BAND_BLOCK
    lane = lax.broadcasted_iota(jnp.int32, (blk, LANES), 1)
    first_head = lane < ATT_HEAD_DIM
    srow = lax.broadcasted_iota(jnp.int32, (LANES, blk), 0)
    first_rows = srow < ATT_HEAD_DIM
    stat_ref[...] = jnp.zeros_like(stat_ref)

    def scores(unit):
        q2 = unit["q"]()
        zero = jnp.zeros_like(q2)
        qst = jnp.concatenate([jnp.where(first_head, q2, zero),
                               jnp.where(first_head, zero, q2)], axis=0)
        return _dot_nt(unit["k"](), qst)

    pending = [scores(u) for u in units[:SCORE_LOOKAHEAD]]
    for idx, unit in enumerate(units):
        st = pending.pop(0)
        if idx + SCORE_LOOKAHEAD < len(units):
            pending.append(scores(units[idx + SCORE_LOOKAHEAD]))
        bias = unit["bias"]()
        ps, mxs = [], []
        for hh in range(2):
            s = st[:, hh * blk:(hh + 1) * blk] + bias
            mx = jnp.max(s, axis=0, keepdims=True)
            ps.append(jnp.exp2(s - mx).astype(BF16))
            mxs.append(mx)
        vt = unit["vt"]()
        ones = jnp.ones((DEN_ROWS, vt.shape[1]), BF16)
        ot = _dot(jnp.concatenate([vt, ones], axis=0), jnp.concatenate(ps, axis=1))
        outs = []
        for hh in range(2):
            den = ot[LANES:LANES + 1, hh * blk:(hh + 1) * blk]
            outs.append(ot[0:LANES, hh * blk:(hh + 1) * blk] * (1.0 / den))
            lse = (mxs[hh] + jnp.log2(den)) * LN2
            for rep in range(HEAD_REP):
                row = rep * ATT_HEADS + 2 * unit["pair"] + hh
                stat_ref[unit["slot"], row:row + 1, :] = lse
        o_t = jnp.where(first_rows, outs[0], outs[1])
        unit["write"](o_t.T.astype(BF16))


def _attn_d1_body(q_ref, kc_ref, kp_ref, vc_ref, vp_ref, bias_ref, o_ref, lse_ref, stat_ref):
    blk = BAND_BLOCK
    n_sub = ATT_ROWS // blk
    has_prev = jnp.where(pl.program_id(1) > 0, 1, 0)
    units = []
    for p in range(N_PAIRS):
        sl = slice(p * LANES, (p + 1) * LANES)
        vts = {}

        def vt_block(m, sl=sl, vts=vts):
            if m not in vts:
                v = vp_ref[:, sl] if m == 0 else vc_ref[(m - 1) * blk:m * blk, sl]
                vts[m] = v.T
            return vts[m]

        for j in range(n_sub):
            def write(o, j=j, sl=sl):
                o_ref[j * blk:(j + 1) * blk, sl] = o

            if j == 0:
                k_of = lambda sl=sl: jnp.concatenate([kp_ref[:, sl], kc_ref[0:blk, sl]], axis=0)
                bias_of = lambda: bias_ref[has_prev]
            else:
                k_of = lambda j=j, sl=sl: kc_ref[(j - 1) * blk:(j + 1) * blk, sl]
                bias_of = lambda: bias_ref[1]
            units.append(dict(
                q=lambda j=j, sl=sl: q_ref[j * blk:(j + 1) * blk, sl], k=k_of,
                vt=lambda j=j, f=vt_block: jnp.concatenate([f(j), f(j + 1)], axis=1),
                bias=bias_of, slot=j, pair=p, write=write))
    _attn_units(units, stat_ref)
    for j in range(n_sub):
        lse_ref[j * blk:(j + 1) * blk, :] = stat_ref[j].T


def _attn_d4_body(q_ref, kc_ref, kp_ref, vc_ref, vp_ref, bias_ref, o_ref, lse_ref, stat_ref):
    blk = BAND_BLOCK
    ni = blk // 4
    has_prev = jnp.where(pl.program_id(1) > 0, 1, 0)
    gather = lambda ref, r, sl: [ref[4 * a + r, :, sl] for a in range(4)]
    units = []
    for p in range(N_PAIRS):
        sl = slice(p * LANES, (p + 1) * LANES)
        for r in range(4):
            def write(o, r=r, sl=sl):
                for a in range(4):
                    o_ref[4 * a + r, :, sl] = o[a * ni:(a + 1) * ni, :]

            units.append(dict(
                q=lambda r=r, sl=sl: jnp.concatenate(gather(q_ref, r, sl), axis=0),
                k=lambda r=r, sl=sl: jnp.concatenate(
                    gather(kp_ref, r, sl) + gather(kc_ref, r, sl), axis=0),
                vt=lambda r=r, sl=sl: jnp.concatenate(
                    [jnp.concatenate(gather(vp_ref, r, sl), axis=0).T,
                     jnp.concatenate(gather(vc_ref, r, sl), axis=0).T], axis=1),
                bias=lambda: bias_ref[has_prev], slot=r, pair=p, write=write))
    _attn_units(units, stat_ref)
    for r in range(4):
        lt = stat_ref[r].T
        for a in range(4):
            lse_ref[4 * a + r] = lt[a * ni:(a + 1) * ni, :]


def _attn_d16_body(q_ref, k_ref, v_ref, bias_ref, o_ref, lse_ref, stat_ref):
    n_sub = q_ref.shape[0]
    units = []
    for p in range(N_PAIRS):
        sl = slice(p * LANES, (p + 1) * LANES)
        for j in range(n_sub):
            def write(o, j=j, sl=sl):
                o_ref[j, :, sl] = o

            units.append(dict(
                q=lambda j=j, sl=sl: q_ref[j, :, sl], k=lambda j=j, sl=sl: k_ref[j, :, sl],
                vt=lambda j=j, sl=sl: v_ref[j, :, sl].T,
                bias=lambda: bias_ref[0], slot=j, pair=p, write=write))
    _attn_units(units, stat_ref)
    for j in range(n_sub):
        lse_ref[j] = stat_ref[j].T


def _stat_scratch(n_sub):
    return [pltpu.VMEM((n_sub, LANES, BAND_BLOCK), F32)]


def _band_bias(kk, qi, need_prev):
    ok = (kk[:, None] >= qi[None, :]) & (kk[:, None] <= qi[None, :] + BAND_BLOCK)
    if need_prev is not None:
        ok = ok & need_prev[:, None]
    return np.where(ok, 0.0, -np.inf).astype(np.float32)


def _attn_d1(q, k, v, batch, seq):
    blk = BAND_BLOCK
    w = ATT_WIDTH
    nt = seq // ATT_ROWS
    per = ATT_ROWS // blk
    kk = np.arange(2 * blk)
    qi = np.arange(blk)
    bias = jnp.asarray(np.stack([_band_bias(kk, qi, kk >= blk), _band_bias(kk, qi, None)]))
    cur = lambda n: pl.BlockSpec((ATT_ROWS, n), lambda b, i: (b * nt + i, 0))
    prev = pl.BlockSpec((blk, w), lambda b, i: (b * nt * per + jnp.maximum(i * per - 1, 0), 0))
    return pl.pallas_call(
        _attn_d1_body, grid=(batch, nt),
        in_specs=[cur(w), cur(w), prev, cur(w), prev, _const_spec(bias.shape)],
        out_specs=[cur(w), cur(LANES)],
        out_shape=[jax.ShapeDtypeStruct((batch * seq, w), BF16),
                   jax.ShapeDtypeStruct((batch * seq, LANES), F32)],
        scratch_shapes=_stat_scratch(per),
        compiler_params=_params(2), name="attn_d1")(q, k, k, v, v, bias)


def _attn_d4(q16, k16, v16, batch, seq):
    blk = BAND_BLOCK
    w = ATT_WIDTH
    ni = blk // 4
    nb = seq // 4 // blk
    pos = np.arange(blk)
    true_i = 4 * (pos % ni) + pos // ni
    kk = np.concatenate([true_i, true_i + blk])
    bias = jnp.asarray(np.stack([_band_bias(kk, true_i, kk >= blk), _band_bias(kk, true_i, None)]))
    cur = lambda n: pl.BlockSpec((None, RES, ni, n), lambda b, i: (b, 0, i, 0))
    prev = pl.BlockSpec((None, RES, ni, w), lambda b, i: (b, 0, jnp.maximum(i - 1, 0), 0))
    return pl.pallas_call(
        _attn_d4_body, grid=(batch, nb),
        in_specs=[cur(w), cur(w), prev, cur(w), prev, _const_spec(bias.shape)],
        out_specs=[cur(w), cur(LANES)],
        out_shape=[jax.ShapeDtypeStruct((batch, RES, seq // RES, w), BF16),
                   jax.ShapeDtypeStruct((batch, RES, seq // RES, LANES), F32)],
        scratch_shapes=_stat_scratch(4),
        compiler_params=_params(2), name="attn_d4")(q16, k16, k16, v16, v16, bias)


def _attn_d16(q16, k16, v16, batch, seq):
    blk = BAND_BLOCK
    w = ATT_WIDTH
    n_sub = 4
    assert seq // RES == blk
    kk = np.arange(blk)
    bias = jnp.asarray(_band_bias(kk + blk, kk, None)[None])
    spec = lambda n: pl.BlockSpec((None, n_sub, blk, n), lambda b, g: (b, g, 0, 0))
    return pl.pallas_call(
        _attn_d16_body, grid=(batch, RES // n_sub),
        in_specs=[spec(w), spec(w), spec(w), _const_spec(bias.shape)],
        out_specs=[spec(w), spec(LANES)],
        out_shape=[jax.ShapeDtypeStruct((batch, RES, blk, w), BF16),
                   jax.ShapeDtypeStruct((batch, RES, blk, LANES), F32)],
        scratch_shapes=_stat_scratch(n_sub),
        compiler_params=_params(2), name="attn_d16")(q16, k16, v16, bias)


def _outproj_body(x_ref, ys_ref, o1_ref, l1_ref, o4_ref, l4_ref, o16_ref, l16_ref, ehead_ref,
                  w_ref, out_ref, small_ref, perm_ref):
    tm = x_ref.shape[0]
    ni = tm // RES

    def to_natural(l_ref):
        for r in range(RES):
            small_ref[pl.ds(r, ni, stride=RES), :] = l_ref[r]
        return small_ref[...]

    def to_l16(val):
        small_ref[...] = val
        return jnp.concatenate([small_ref[pl.ds(r, ni, stride=RES), :] for r in range(RES)], axis=0)

    l1 = l1_ref[...]
    l4 = to_natural(l4_ref)
    l16 = to_natural(l16_ref)
    m = jnp.maximum(jnp.maximum(l1, l4), l16)
    e1, e4, e16 = jnp.exp(l1 - m), jnp.exp(l4 - m), jnp.exp(l16 - m)
    inv = 1.0 / (e1 + e4 + e16)
    a1 = _split3_lanes(e1 * inv)
    a4 = _split3_lanes(to_l16(e4 * inv))
    a16 = _split3_lanes(to_l16(e16 * inv))
    acc = x_ref[...] + _dot(ys_ref[...], w_ref[0:SSD_WIDTH, :])
    per = MXU_COLS // LANES
    for c in range(ATT_WIDTH // MXU_COLS):
        cols = slice(c * MXU_COLS, (c + 1) * MXU_COLS)
        ehead = ehead_ref[:, cols]
        o4 = o4_ref[:, :, cols].reshape(tm, MXU_COLS).astype(F32)
        o16 = o16_ref[:, :, cols].reshape(tm, MXU_COLS).astype(F32)
        y_l16 = _dot(a4, ehead) * o4 + _dot(a16, ehead) * o16
        for r in range(RES):
            for s in range(per):
                perm_ref[c * per + s, pl.ds(r, ni, stride=RES), :] = (
                    y_l16[r * ni:(r + 1) * ni, s * LANES:(s + 1) * LANES])
        y_nat = jnp.concatenate([perm_ref[c * per + s] for s in range(per)], axis=1)
        y_att = _dot(a1, ehead) * o1_ref[:, cols].astype(F32) + y_nat
        acc = acc + _dot(y_att.astype(BF16),
                         w_ref[SSD_WIDTH + c * MXU_COLS:SSD_WIDTH + (c + 1) * MXU_COLS, :])
    out_ref[...] = acc


def _outproj(x, y_ssd, o1, l1, o4, l4, o16, l16, ehead, w, seq):
    t, d = x.shape
    tps = seq // TM
    ni = TM // RES
    row = lambda n: pl.BlockSpec((TM, n), lambda i: (i, 0))
    slab = lambda n: pl.BlockSpec((None, RES, ni, n), lambda i: (i // tps, 0, i % tps, 0))
    return pl.pallas_call(
        _outproj_body, grid=(t // TM,),
        in_specs=[row(d), row(SSD_WIDTH), row(ATT_WIDTH), row(LANES), slab(ATT_WIDTH), slab(LANES),
                  slab(ATT_WIDTH), slab(LANES), _const_spec((LANES, ATT_WIDTH)),
                  _const_spec((SSD_WIDTH + ATT_WIDTH, d))],
        out_specs=row(d), out_shape=jax.ShapeDtypeStruct((t, d), F32),
        scratch_shapes=[pltpu.VMEM((TM, LANES), F32),
                        pltpu.VMEM((ATT_WIDTH // LANES, TM, LANES), F32)],
        compiler_params=_params(1), name="outproj")(x, y_ssd, o1, l1, o4, l4, o16, l16, ehead, w)


def _head_expand(width_per_head):
    r = jnp.arange(LANES)[:, None]
    c = jnp.arange(SSD_HEADS * width_per_head)[None, :]
    return ((r < HEAD_REP * SSD_HEADS) & (r % SSD_HEADS == c // width_per_head)).astype(BF16)


def _rep_heads(v):
    return jnp.pad(jnp.tile(v.astype(F32), HEAD_REP), (0, LANES - HEAD_REP * SSD_HEADS))[None, :]


def kernel(x, ffn1_norm, ffn1_w_gate, ffn1_w_up, ffn1_w_down, mix_norm, w_in, conv_w, conv_b,
           dt_bias, a_log, d_skip, ssd_norm, q_norm, k_norm, w_out, ffn2_norm, ffn2_w_gate,
           ffn2_w_up, ffn2_w_down):
    batch, seq, d = x.shape
    depth = w_in.shape[0]
    assert d == D_MODEL and seq == RES * BAND_BLOCK
    assert all(wd // dl == BAND_BLOCK for wd, dl in ATT_BRANCHES)
    assert seq % TM == 0 and seq % ATT_ROWS == 0 and seq % SSD_ROWS == 0
    t = batch * seq

    tril = jnp.tril(jnp.ones((SSD_CHUNK, SSD_CHUNK), BF16))
    eye = jnp.eye(SSD_CHUNK, dtype=BF16)
    ehead = _head_expand(SSD_HEAD_DIM)
    masked = jnp.where(tril > 0, 0.0, MASKED).astype(BF16)
    ecol = jnp.concatenate([_head_expand(SSD_CHUNK), jnp.tile(masked, (1, SSD_HEADS))], axis=0)
    half = jnp.arange(LANES) // ATT_HEAD_DIM
    bd = (half[:, None] == half[None, :]).astype(F32) / ATT_HEAD_DIM
    bd2 = jnp.concatenate([bd, bd], axis=0).astype(BF16)
    scale = LOG2E / math.sqrt(ATT_HEAD_DIM)

    xf = x.reshape(t, d)
    for i in range(depth):
        dt_cols = jnp.pad(jnp.tile(w_in[i][:, COL_DT:COL_DT + SSD_HEADS], (1, HEAD_REP)),
                          ((0, 0), (0, LANES - HEAD_REP * SSD_HEADS)))
        w_ssd = jnp.concatenate([w_in[i][:, :COL_DT], dt_cols], axis=1).astype(BF16)
        w_att = w_in[i][:, COL_DT + SSD_HEADS:].astype(BF16)
        qg = jnp.tile(q_norm[i].astype(F32) * scale, LANES // ATT_HEAD_DIM)[None, :]
        kg = jnp.tile(k_norm[i].astype(F32), LANES // ATT_HEAD_DIM)[None, :]

        xf = _ffn(xf, ffn1_norm[i][None, :], ffn1_w_gate[i].astype(BF16),
                  ffn1_w_up[i].astype(BF16), ffn1_w_down[i].astype(BF16))
        (qn, kn, vn, q16, k16, v16, z, xs, bc, dt) = _inproj(
            xf, mix_norm[i][None, :], w_att, w_ssd, qg, kg, bd2, conv_w[i], conv_b[i][None, :],
            batch, seq)
        y_ssd = _ssd(xs, bc, dt, z, _rep_heads(dt_bias[i]), _rep_heads(a_log[i]),
                     jnp.repeat(d_skip[i].astype(F32), SSD_HEAD_DIM)[None, :],
                     ssd_norm[i][None, :], tril, eye, ehead, ecol, batch, seq)
        o1, l1 = _attn_d1(qn, kn, vn, batch, seq)
        o4, l4 = _attn_d4(q16, k16, v16, batch, seq)
        o16, l16 = _attn_d16(q16, k16, v16, batch, seq)
        xf = _outproj(xf, y_ssd, o1, l1, o4, l4, o16, l16, ehead, w_out[i].astype(BF16), seq)
        xf = _ffn(xf, ffn2_norm[i][None, :], ffn2_w_gate[i].astype(BF16),
                  ffn2_w_up[i].astype(BF16), ffn2_w_down[i].astype(BF16))
    return xf.reshape(batch, seq, d)
```

```python
import functools
import math

import numpy as np
import jax
import jax.numpy as jnp
from jax import lax
from jax.experimental import pallas as pl
from jax.experimental.pallas import tpu as pltpu

F32 = jnp.float32
BF16 = jnp.bfloat16

D_MODEL = 1024
SSD_HEADS = 16
SSD_HEAD_DIM = 64
SSD_WIDTH = SSD_HEADS * SSD_HEAD_DIM
SSD_GROUPS = 4
SSD_STATE = 128
CONV_K = 4
SSD_CHUNK = 128
BC_WIDTH = 2 * SSD_GROUPS * SSD_STATE
CONV_DIM = SSD_WIDTH + BC_WIDTH
ATT_HEADS = 16
ATT_HEAD_DIM = 64
ATT_WIDTH = ATT_HEADS * ATT_HEAD_DIM
ATT_BRANCHES = ((128, 1), (512, 4), (2048, 16))
BAND_BLOCK = 128
EPS = 1e-6

LANES = 128
SUBLANES = 8
MXU_COLS = 256
VMEM_LIMIT_BYTES = 56 * 1024 * 1024

LOG2E = math.log2(math.e)
LN2 = math.log(2.0)
MASKED = -1e30
DEN_ROWS = 16
SCORE_LOOKAHEAD = 6
DENSE_LOOKAHEAD = 2

HEAD_REP = 3
RES = 16
N_PAIRS = ATT_HEADS // 2

COL_Z = 0
COL_XBC = COL_Z + SSD_WIDTH
COL_DT = COL_XBC + CONV_DIM
SSD_COLS = COL_DT + LANES

TM = 1024
ATT_ROWS = 1024
SSD_ROWS = 1024

def _const_spec(shape):
    nd = len(shape)
    return pl.BlockSpec(shape, lambda *_: (0,) * nd, pipeline_mode=pl.Buffered(1))


def _params(n_axes):
    return pltpu.CompilerParams(dimension_semantics=("arbitrary",) * n_axes,
                                vmem_limit_bytes=VMEM_LIMIT_BYTES)


def _rms(x, gain):
    ms = jnp.mean(x * x, axis=-1, keepdims=True)
    return x * lax.rsqrt(ms + EPS) * gain


def _silu(x):
    h = 0.5 * x
    return h + h * jnp.tanh(h)


def _dot(a, b):
    return jnp.dot(a, b, preferred_element_type=F32)


def _dot_nt(a, b):
    return lax.dot_general(a, b, (((1,), (1,)), ((), ())), preferred_element_type=F32)


def _dot_tn(a, b):
    return lax.dot_general(a, b, (((0,), (0,)), ((), ())), preferred_element_type=F32)


def _split3(v):
    hi = v.astype(BF16)
    r1 = v - hi.astype(F32)
    mid = r1.astype(BF16)
    lo = (r1 - mid.astype(F32)).astype(BF16)
    return hi, mid, lo


def _split3_lanes(v):
    hi, mid, lo = _split3(v)
    lane = lax.broadcasted_iota(jnp.int32, v.shape, 1)
    zero = jnp.zeros_like(hi)
    return jnp.where(lane < SSD_HEADS, hi,
                     jnp.where(lane < 2 * SSD_HEADS, mid,
                               jnp.where(lane < 3 * SSD_HEADS, lo, zero)))


def _ffn_body(x_ref, g_ref, wg_ref, wu_ref, wd_ref, o_ref):
    x = x_ref[...]
    xn = _rms(x, g_ref[...]).astype(BF16)
    gate = _dot(xn, wg_ref[...])
    up = _dot(xn, wu_ref[...])
    h = (_silu(gate) * up).astype(BF16)
    o_ref[...] = x + 0.5 * _dot(h, wd_ref[...])


def _ffn(x, gain, wg, wu, wd):
    t, d = x.shape
    f = wg.shape[1]
    row = pl.BlockSpec((TM, d), lambda i: (i, 0))
    return pl.pallas_call(
        _ffn_body, grid=(t // TM,),
        in_specs=[row, _const_spec((1, d)), _const_spec((d, f)), _const_spec((d, f)),
                  _const_spec((f, d))],
        out_specs=row, out_shape=jax.ShapeDtypeStruct((t, d), F32),
        compiler_params=_params(1), name="ffn")(x, gain, wg, wu, wd)


def _proj_ssd_body(x_ref, g_ref, w_ref, cw_ref, cb_ref, z_ref, xs_ref, bc_ref, dt_ref, cbuf_ref,
                   xn_ref, *, tiles_per_seq):
    tm = x_ref.shape[0]
    n_chunks = CONV_DIM // MXU_COLS
    per = MXU_COLS // LANES
    xn_ref[...] = _rms(x_ref[...], g_ref[...]).astype(BF16)

    @pl.when(pl.program_id(0) % tiles_per_seq == 0)
    def _():
        cbuf_ref[:, 0:SUBLANES, :] = jnp.zeros((CONV_DIM // LANES, SUBLANES, LANES), F32)

    main = lambda c: _dot(
        xn_ref[...], w_ref[:, COL_XBC + c * MXU_COLS:COL_XBC + (c + 1) * MXU_COLS])
    z_ref[...] = _dot(xn_ref[...], w_ref[:, COL_Z:COL_XBC]).astype(BF16)
    dt_ref[...] = _dot(xn_ref[...], w_ref[:, COL_DT:SSD_COLS])
    pending = [main(c) for c in range(DENSE_LOOKAHEAD)]
    for c in range(n_chunks):
        res = pending.pop(0)
        if c + DENSE_LOOKAHEAD < n_chunks:
            pending.append(main(c + DENSE_LOOKAHEAD))
        for s in range(per):
            slab = c * per + s
            lanes = slice(slab * LANES, (slab + 1) * LANES)
            cbuf_ref[slab, SUBLANES:SUBLANES + tm, :] = res[:, s * LANES:(s + 1) * LANES]
            conv = cb_ref[:, lanes]
            for k in range(CONV_K):
                off = SUBLANES - (CONV_K - 1) + k
                conv = conv + cw_ref[k:k + 1, lanes] * cbuf_ref[slab, off:off + tm, :]
            cbuf_ref[slab, 0:SUBLANES, :] = cbuf_ref[slab, tm:tm + SUBLANES, :]
            act = _silu(conv)
            if slab * LANES < SSD_WIDTH:
                xs_ref[:, lanes] = act
            else:
                bc_ref[:, slab * LANES - SSD_WIDTH:(slab + 1) * LANES - SSD_WIDTH] = act.astype(BF16)


def _proj_ssd(x, gain, w, cw, cb, seq):
    t, d = x.shape
    row = lambda n: pl.BlockSpec((TM, n), lambda i: (i, 0))
    shp = lambda n, dt: jax.ShapeDtypeStruct((t, n), dt)
    return pl.pallas_call(
        functools.partial(_proj_ssd_body, tiles_per_seq=seq // TM), grid=(t // TM,),
        in_specs=[row(d), _const_spec((1, d)), _const_spec((d, SSD_COLS)),
                  _const_spec((CONV_K, CONV_DIM)), _const_spec((1, CONV_DIM))],
        out_specs=[row(SSD_WIDTH), row(SSD_WIDTH), row(BC_WIDTH), row(LANES)],
        out_shape=[shp(SSD_WIDTH, BF16), shp(SSD_WIDTH, F32), shp(BC_WIDTH, BF16),
                   shp(LANES, F32)],
        scratch_shapes=[pltpu.VMEM((CONV_DIM // LANES, SUBLANES + TM, LANES), F32),
                        pltpu.VMEM((TM, d), BF16)],
        compiler_params=_params(1), name="proj_ssd")(x, gain, w, cw, cb)


def _proj_att_body(x_ref, g_ref, w_ref, qg_ref, kg_ref, bd2_ref,
                   qn_ref, kn_ref, vn_ref, q16_ref, k16_ref, v16_ref, scr_ref, xn_ref):
    tm = x_ref.shape[0]
    ni = tm // RES
    per = MXU_COLS // LANES
    n_slabs = scr_ref.shape[0]
    xn_ref[...] = _rms(x_ref[...], g_ref[...]).astype(BF16)
    bd2 = bd2_ref[...]
    plan = ((qg_ref, qn_ref, q16_ref), (kg_ref, kn_ref, k16_ref), (None, vn_ref, v16_ref))
    chunks = [(t, c) for t in range(3) for c in range(ATT_WIDTH // MXU_COLS)]
    main = lambda t, c: _dot(
        xn_ref[...], w_ref[:, t * ATT_WIDTH + c * MXU_COLS:t * ATT_WIDTH + (c + 1) * MXU_COLS])
    pending = [main(*ch) for ch in chunks[:DENSE_LOOKAHEAD]]
    for ci, (t, c) in enumerate(chunks):
        gain_ref, nat_ref, l16_ref = plan[t]
        res = pending.pop(0)
        if ci + DENSE_LOOKAHEAD < len(chunks):
            pending.append(main(*chunks[ci + DENSE_LOOKAHEAD]))
        for s in range(per):
            lanes = slice(c * MXU_COLS + s * LANES, c * MXU_COLS + (s + 1) * LANES)
            y = res[:, s * LANES:(s + 1) * LANES]
            if gain_ref is not None:
                sq = y * y
                hi = sq.astype(BF16)
                lo = (sq - hi.astype(F32)).astype(BF16)
                ms = _dot(jnp.concatenate([hi, lo], axis=1), bd2)
                y = y * lax.rsqrt(ms + EPS) * gain_ref[...]
            nat_ref[:, lanes] = y.astype(BF16)
            slab = (ci * per + s) % n_slabs
            scr_ref[slab] = y
            for r in range(RES):
                l16_ref[r, :, lanes] = scr_ref[slab, pl.ds(r, ni, stride=RES), :].astype(BF16)


def _proj_att(x, gain, w, qg, kg, bd2, batch, seq):
    t, d = x.shape
    tps = seq // TM
    ni = TM // RES
    row = lambda n: pl.BlockSpec((TM, n), lambda i: (i, 0))
    l16 = pl.BlockSpec((None, RES, ni, ATT_WIDTH), lambda i: (i // tps, 0, i % tps, 0))
    nat_shape = jax.ShapeDtypeStruct((t, ATT_WIDTH), BF16)
    l16_shape = jax.ShapeDtypeStruct((batch, RES, seq // RES, ATT_WIDTH), BF16)
    return pl.pallas_call(
        _proj_att_body, grid=(t // TM,),
        in_specs=[row(d), _const_spec((1, d)), _const_spec((d, 3 * ATT_WIDTH)),
                  _const_spec((1, LANES)), _const_spec((1, LANES)),
                  _const_spec((2 * LANES, LANES))],
        out_specs=[row(ATT_WIDTH)] * 3 + [l16] * 3,
        out_shape=[nat_shape] * 3 + [l16_shape] * 3,
        scratch_shapes=[pltpu.VMEM((4, TM, LANES), F32), pltpu.VMEM((TM, d), BF16)],
        compiler_params=_params(1), name="proj_att")(x, gain, w, qg, kg, bd2)


def _ssd_body(xs_ref, bc_ref, dt_ref, z_ref, dtb_ref, alog_ref, dskip_ref, nw_ref,
              tril_ref, eye_ref, ehead_ref, ecol_ref, y_ref, state_ref, acumt_ref, *, chunks):
    L = SSD_CHUNK
    gsz = SSD_GROUPS * SSD_STATE

    @pl.when(pl.program_id(1) == 0)
    def _():
        state_ref[...] = jnp.zeros_like(state_ref)

    lane = lax.broadcasted_iota(jnp.int32, (L, SSD_WIDTH), 1)
    even_head = (lane // SSD_HEAD_DIM) % 2 == 0
    heads_per_group = SSD_HEADS // SSD_GROUPS
    gw = heads_per_group * SSD_HEAD_DIM
    groups = range(SSD_GROUPS)

    def front(ci):
        rows = slice(ci * L, (ci + 1) * L)
        cgs = [bc_ref[rows, gsz + g * SSD_STATE:gsz + (g + 1) * SSD_STATE] for g in groups]
        bgs = [bc_ref[rows, g * SSD_STATE:(g + 1) * SSD_STATE] for g in groups]
        cbms = [_dot_nt(cgs[g], bgs[g]) for g in groups]
        dt = jax.nn.softplus(dt_ref[rows, :] + dtb_ref[...])
        adt = dt * (-jnp.exp(alog_ref[...]) * LOG2E)
        tril = tril_ref[...]
        hi, mid, lo = _split3(adt)
        acum = _dot(tril, hi) + _dot(tril, mid) + _dot(tril, lo)
        acumt_ref[ci] = acum.T
        dt_x = _dot(_split3_lanes(dt), ehead_ref[...])
        acum3 = _split3_lanes(acum)
        acum_x = _dot(acum3, ehead_ref[...])
        acum_col = _dot(jnp.concatenate([acum3, eye_ref[...]], axis=1), ecol_ref[...])
        return cgs, bgs, cbms, dt_x, acum_x, acum_col

    def back(ci, cgs, bgs, cbms, dt_x, acum_x, acum_col):
        rows = slice(ci * L, (ci + 1) * L)
        xs = xs_ref[rows, :]
        sts = [state_ref[:, g * gw:(g + 1) * gw] for g in groups]
        y_offs = [_dot(cgs[g], sts[g].astype(BF16)) for g in groups]
        xdt = xs * dt_x
        eacum_x = jnp.exp2(acum_x)
        tot_x = acum_x[L - 1:L, :]
        xdt_b = xdt.astype(BF16)
        zero = jnp.zeros_like(xdt_b)
        xdt_even = jnp.where(even_head, xdt_b, zero)
        xdt_odd = jnp.where(even_head, zero, xdt_b)
        xdec_b = (xdt * jnp.exp2(tot_x - acum_x)).astype(BF16)
        chunk_decay = eacum_x[L - 1:L, :]
        for g in groups:
            gs = slice(g * gw, (g + 1) * gw)
            new_states = _dot_tn(bgs[g], xdec_b[:, gs])
            state_ref[:, gs] = sts[g] * chunk_decay[:, gs] + new_states
        y_parts = []
        for g in groups:
            gs = slice(g * gw, (g + 1) * gw)
            ms = []
            for r in range(heads_per_group):
                h = g * heads_per_group + r
                seg = acum_col[:, h * L:(h + 1) * L] - acumt_ref[ci, h:h + 1, :]
                ms.append((cbms[g] * jnp.exp2(seg)).astype(BF16))
            yd = []
            for pr in range(heads_per_group // 2):
                pair = g * (heads_per_group // 2) + pr
                sl = slice(pair * LANES, (pair + 1) * LANES)
                lhs = jnp.concatenate([ms[2 * pr], ms[2 * pr + 1]], axis=1)
                rhs = jnp.concatenate([xdt_even[:, sl], xdt_odd[:, sl]], axis=0)
                yd.append(_dot(lhs, rhs))
            y_parts.append(jnp.concatenate(yd, axis=1) + y_offs[g] * eacum_x[:, gs])

        y = jnp.concatenate(y_parts, axis=1) + xs * dskip_ref[...]
        y = y * _silu(z_ref[rows, :].astype(F32))
        outs = []
        for g in groups:
            yg = y[:, g * gw:(g + 1) * gw]
            msq = jnp.mean(yg * yg, axis=-1, keepdims=True)
            outs.append(yg * lax.rsqrt(msq + EPS))
        y_ref[rows, :] = (jnp.concatenate(outs, axis=1) * nw_ref[...]).astype(BF16)

    pending = front(0)
    for ci in range(chunks):
        ready = pending
        if ci + 1 < chunks:
            pending = front(ci + 1)
        back(ci, *ready)


def _ssd(xs, bc, dt, z, dtb, alog, dskip_x, nw, tril, eye, ehead, ecol, batch, seq):
    L = SSD_CHUNK
    ns = seq // SSD_ROWS
    chunks = SSD_ROWS // L
    blk = lambda n: pl.BlockSpec((SSD_ROWS, n), lambda b, s: (b * ns + s, 0))
    return pl.pallas_call(
        functools.partial(_ssd_body, chunks=chunks), grid=(batch, ns),
        in_specs=[blk(SSD_WIDTH), blk(BC_WIDTH), blk(LANES), blk(SSD_WIDTH),
                  _const_spec((1, LANES)), _const_spec((1, LANES)),
                  _const_spec((1, SSD_WIDTH)), _const_spec((1, SSD_WIDTH)),
                  _const_spec((L, L)), _const_spec((L, L)), _const_spec((LANES, SSD_WIDTH)),
                  _const_spec((LANES + L, SSD_HEADS * L))],
        out_specs=blk(SSD_WIDTH),
        out_shape=jax.ShapeDtypeStruct((batch * seq, SSD_WIDTH), BF16),
        scratch_shapes=[pltpu.VMEM((SSD_STATE, SSD_WIDTH), F32),
                        pltpu.VMEM((chunks, L, LANES), F32)],
        compiler_params=_params(2), name="ssd")(
            xs, bc, dt, z, dtb, alog, dskip_x, nw, tril, eye, ehead, ecol)


def _attn_units(units, stat_ref, st_ref):
    blk = BAND_BLOCK
    lane = lax.broadcasted_iota(jnp.int32, (blk, LANES), 1)
    first_head = lane < ATT_HEAD_DIM
    srow = lax.broadcasted_iota(jnp.int32, (LANES, blk), 0)
    first_rows = srow < ATT_HEAD_DIM
    stat_ref[...] = jnp.zeros_like(stat_ref)

    def scores(unit):
        q2 = unit["q"]()
        zero = jnp.zeros_like(q2)
        qst = jnp.concatenate([jnp.where(first_head, q2, zero),
                               jnp.where(first_head, zero, q2)], axis=0)
        return _dot_nt(unit["k"](), qst)

    n_slots = st_ref.shape[0]

    def issue(idx):
        st_ref[idx % n_slots] = scores(units[idx])

    for idx in range(min(SCORE_LOOKAHEAD, len(units))):
        issue(idx)
    for idx, unit in enumerate(units):
        if idx + SCORE_LOOKAHEAD < len(units):
            issue(idx + SCORE_LOOKAHEAD)
        bias = unit["bias"]()
        ps, mxs = [], []
        for hh in range(2):
            s = st_ref[idx % n_slots, :, hh * blk:(hh + 1) * blk] + bias
            mx = jnp.max(s, axis=0, keepdims=True)
            ps.append(jnp.exp2(s - mx).astype(BF16))
            mxs.append(mx)
        vt = unit["vt"]()
        ones = jnp.ones((DEN_ROWS, vt.shape[1]), BF16)
        ot = _dot(jnp.concatenate([vt, ones], axis=0), jnp.concatenate(ps, axis=1))
        outs = []
        for hh in range(2):
            den = ot[LANES:LANES + 1, hh * blk:(hh + 1) * blk]
            outs.append(ot[0:LANES, hh * blk:(hh + 1) * blk] * (1.0 / den))
            lse = (mxs[hh] + jnp.log2(den)) * LN2
            for rep in range(HEAD_REP):
                row = rep * ATT_HEADS + 2 * unit["pair"] + hh
                stat_ref[unit["slot"], row:row + 1, :] = lse
        o_t = jnp.where(first_rows, outs[0], outs[1])
        unit["write"](o_t.T.astype(BF16))


def _attn_d1_body(q_ref, kc_ref, kp_ref, vc_ref, vp_ref, bias_ref, o_ref, lse_ref, stat_ref,
                  st_ref):
    blk = BAND_BLOCK
    n_sub = ATT_ROWS // blk
    has_prev = jnp.where(pl.program_id(1) > 0, 1, 0)
    units = []
    for p in range(N_PAIRS):
        sl = slice(p * LANES, (p + 1) * LANES)
        vts = {}

        def vt_block(m, sl=sl, vts=vts):
            if m not in vts:
                v = vp_ref[:, sl] if m == 0 else vc_ref[(m - 1) * blk:m * blk, sl]
                vts[m] = v.T
            return vts[m]

        for j in range(n_sub):
            def write(o, j=j, sl=sl):
                o_ref[j * blk:(j + 1) * blk, sl] = o

            if j == 0:
                k_of = lambda sl=sl: jnp.concatenate([kp_ref[:, sl], kc_ref[0:blk, sl]], axis=0)
                bias_of = lambda: bias_ref[has_prev]
            else:
                k_of = lambda j=j, sl=sl: kc_ref[(j - 1) * blk:(j + 1) * blk, sl]
                bias_of = lambda: bias_ref[1]
            units.append(dict(
                q=lambda j=j, sl=sl: q_ref[j * blk:(j + 1) * blk, sl], k=k_of,
                vt=lambda j=j, f=vt_block: jnp.concatenate([f(j), f(j + 1)], axis=1),
                bias=bias_of, slot=j, pair=p, write=write))
    _attn_units(units, stat_ref, st_ref)
    for j in range(n_sub):
        lse_ref[j * blk:(j + 1) * blk, :] = stat_ref[j].T


def _attn_d4_body(q_ref, kc_ref, kp_ref, vc_ref, vp_ref, bias_ref, o_ref, lse_ref, stat_ref,
                  st_ref):
    blk = BAND_BLOCK
    ni = blk // 4
    n_groups = q_ref.shape[1] // ni
    has_prev = jnp.where(pl.program_id(1) > 0, 1, 0)
    rows = lambda g: slice(g * ni, (g + 1) * ni)
    gather = lambda ref, r, g, sl: [ref[4 * a + r, rows(g), sl] for a in range(4)]

    def window(cur_ref, prev_ref, r, g, sl):
        before = gather(prev_ref, r, 0, sl) if g == 0 else gather(cur_ref, r, g - 1, sl)
        return jnp.concatenate(before, axis=0), jnp.concatenate(gather(cur_ref, r, g, sl), axis=0)

    units = []
    for p in range(N_PAIRS):
        sl = slice(p * LANES, (p + 1) * LANES)
        for g in range(n_groups):
            for r in range(4):
                def write(o, r=r, g=g, sl=sl):
                    for a in range(4):
                        o_ref[4 * a + r, rows(g), sl] = o[a * ni:(a + 1) * ni, :]

                units.append(dict(
                    q=lambda r=r, g=g, sl=sl: jnp.concatenate(gather(q_ref, r, g, sl), axis=0),
                    k=lambda r=r, g=g, sl=sl: jnp.concatenate(
                        window(kc_ref, kp_ref, r, g, sl), axis=0),
                    vt=lambda r=r, g=g, sl=sl: jnp.concatenate(
                        [v.T for v in window(vc_ref, vp_ref, r, g, sl)], axis=1),
                    bias=(lambda: bias_ref[has_prev]) if g == 0 else (lambda: bias_ref[1]),
                    slot=g * 4 + r, pair=p, write=write))
    _attn_units(units, stat_ref, st_ref)
    for g in range(n_groups):
        for r in range(4):
            lt = stat_ref[g * 4 + r].T
            for a in range(4):
                lse_ref[4 * a + r, rows(g), :] = lt[a * ni:(a + 1) * ni, :]


def _attn_d16_body(q_ref, k_ref, v_ref, bias_ref, o_ref, lse_ref, stat_ref, st_ref):
    n_sub = q_ref.shape[0]
    units = []
    for p in range(N_PAIRS):
        sl = slice(p * LANES, (p + 1) * LANES)
        for j in range(n_sub):
            def write(o, j=j, sl=sl):
                o_ref[j, :, sl] = o

            units.append(dict(
                q=lambda j=j, sl=sl: q_ref[j, :, sl], k=lambda j=j, sl=sl: k_ref[j, :, sl],
                vt=lambda j=j, sl=sl: v_ref[j, :, sl].T,
                bias=lambda: bias_ref[0], slot=j, pair=p, write=write))
    _attn_units(units, stat_ref, st_ref)
    for j in range(n_sub):
        lse_ref[j] = stat_ref[j].T


def _attn_scratch(n_sub, nk):
    return [pltpu.VMEM((n_sub, LANES, BAND_BLOCK), F32),
            pltpu.VMEM((SCORE_LOOKAHEAD + 1, nk, 2 * BAND_BLOCK), F32)]


def _band_bias(kk, qi, need_prev):
    ok = (kk[:, None] >= qi[None, :]) & (kk[:, None] <= qi[None, :] + BAND_BLOCK)
    if need_prev is not None:
        ok = ok & need_prev[:, None]
    return np.where(ok, 0.0, -np.inf).astype(np.float32)


def _attn_d1(q, k, v, batch, seq):
    blk = BAND_BLOCK
    w = ATT_WIDTH
    nt = seq // ATT_ROWS
    per = ATT_ROWS // blk
    kk = np.arange(2 * blk)
    qi = np.arange(blk)
    bias = jnp.asarray(np.stack([_band_bias(kk, qi, kk >= blk), _band_bias(kk, qi, None)]))
    cur = lambda n: pl.BlockSpec((ATT_ROWS, n), lambda b, i: (b * nt + i, 0))
    prev = pl.BlockSpec((blk, w), lambda b, i: (b * nt * per + jnp.maximum(i * per - 1, 0), 0))
    return pl.pallas_call(
        _attn_d1_body, grid=(batch, nt),
        in_specs=[cur(w), cur(w), prev, cur(w), prev, _const_spec(bias.shape)],
        out_specs=[cur(w), cur(LANES)],
        out_shape=[jax.ShapeDtypeStruct((batch * seq, w), BF16),
                   jax.ShapeDtypeStruct((batch * seq, LANES), F32)],
        scratch_shapes=_attn_scratch(per, 2 * blk),
        compiler_params=_params(2), name="attn_d1")(q, k, k, v, v, bias)


def _attn_d4(q16, k16, v16, batch, seq):
    blk = BAND_BLOCK
    w = ATT_WIDTH
    ni = blk // 4
    nb = seq // 4 // blk
    pos = np.arange(blk)
    true_i = 4 * (pos % ni) + pos // ni
    kk = np.concatenate([true_i, true_i + blk])
    bias = jnp.asarray(np.stack([_band_bias(kk, true_i, kk >= blk), _band_bias(kk, true_i, None)]))
    per = ATT_ROWS // (4 * blk)
    cur = lambda n: pl.BlockSpec((None, RES, per * ni, n), lambda b, i: (b, 0, i, 0))
    prev = pl.BlockSpec((None, RES, ni, w), lambda b, i: (b, 0, jnp.maximum(i * per - 1, 0), 0))
    return pl.pallas_call(
        _attn_d4_body, grid=(batch, nb // per),
        in_specs=[cur(w), cur(w), prev, cur(w), prev, _const_spec(bias.shape)],
        out_specs=[cur(w), cur(LANES)],
        out_shape=[jax.ShapeDtypeStruct((batch, RES, seq // RES, w), BF16),
                   jax.ShapeDtypeStruct((batch, RES, seq // RES, LANES), F32)],
        scratch_shapes=_attn_scratch(4 * per, 2 * blk),
        compiler_params=_params(2), name="attn_d4")(q16, k16, k16, v16, v16, bias)


def _attn_d16(q16, k16, v16, batch, seq):
    blk = BAND_BLOCK
    w = ATT_WIDTH
    n_sub = ATT_ROWS // blk
    assert seq // RES == blk and RES % n_sub == 0
    kk = np.arange(blk)
    bias = jnp.asarray(_band_bias(kk + blk, kk, None)[None])
    spec = lambda n: pl.BlockSpec((None, n_sub, blk, n), lambda b, g: (b, g, 0, 0))
    return pl.pallas_call(
        _attn_d16_body, grid=(batch, RES // n_sub),
        in_specs=[spec(w), spec(w), spec(w), _const_spec(bias.shape)],
        out_specs=[spec(w), spec(LANES)],
        out_shape=[jax.ShapeDtypeStruct((batch, RES, blk, w), BF16),
                   jax.ShapeDtypeStruct((batch, RES, blk, LANES), F32)],
        scratch_shapes=_attn_scratch(n_sub, blk),
        compiler_params=_params(2), name="attn_d16")(q16, k16, v16, bias)


def _outproj_body(x_ref, ys_ref, o1_ref, l1_ref, o4_ref, l4_ref, o16_ref, l16_ref, ehead_ref,
                  w_ref, out_ref, small_ref, perm_ref):
    tm = x_ref.shape[0]
    ni = tm // RES

    def to_natural(l_ref):
        for r in range(RES):
            small_ref[pl.ds(r, ni, stride=RES), :] = l_ref[r]
        return small_ref[...]

    def to_l16(val):
        small_ref[...] = val
        return jnp.concatenate([small_ref[pl.ds(r, ni, stride=RES), :] for r in range(RES)], axis=0)

    l1 = l1_ref[...]
    l4 = to_natural(l4_ref)
    l16 = to_natural(l16_ref)
    m = jnp.maximum(jnp.maximum(l1, l4), l16)
    e1, e4, e16 = jnp.exp(l1 - m), jnp.exp(l4 - m), jnp.exp(l16 - m)
    inv = 1.0 / (e1 + e4 + e16)
    a1 = _split3_lanes(e1 * inv)
    a4 = _split3_lanes(to_l16(e4 * inv))
    a16 = _split3_lanes(to_l16(e16 * inv))
    acc = x_ref[...] + _dot(ys_ref[...], w_ref[0:SSD_WIDTH, :])
    per = MXU_COLS // LANES
    for c in range(ATT_WIDTH // MXU_COLS):
        cols = slice(c * MXU_COLS, (c + 1) * MXU_COLS)
        ehead = ehead_ref[:, cols]
        o4 = o4_ref[:, :, cols].reshape(tm, MXU_COLS).astype(F32)
        o16 = o16_ref[:, :, cols].reshape(tm, MXU_COLS).astype(F32)
        y_l16 = _dot(a4, ehead) * o4 + _dot(a16, ehead) * o16
        for r in range(RES):
            for s in range(per):
                perm_ref[c * per + s, pl.ds(r, ni, stride=RES), :] = (
                    y_l16[r * ni:(r + 1) * ni, s * LANES:(s + 1) * LANES])
        y_nat = jnp.concatenate([perm_ref[c * per + s] for s in range(per)], axis=1)
        y_att = _dot(a1, ehead) * o1_ref[:, cols].astype(F32) + y_nat
        acc = acc + _dot(y_att.astype(BF16),
                         w_ref[SSD_WIDTH + c * MXU_COLS:SSD_WIDTH + (c + 1) * MXU_COLS, :])
    out_ref[...] = acc


def _outproj(x, y_ssd, o1, l1, o4, l4, o16, l16, ehead, w, seq):
    t, d = x.shape
    tps = seq // TM
    ni = TM // RES
    row = lambda n: pl.BlockSpec((TM, n), lambda i: (i, 0))
    slab = lambda n: pl.BlockSpec((None, RES, ni, n), lambda i: (i // tps, 0, i % tps, 0))
    return pl.pallas_call(
        _outproj_body, grid=(t // TM,),
        in_specs=[row(d), row(SSD_WIDTH), row(ATT_WIDTH), row(LANES), slab(ATT_WIDTH), slab(LANES),
                  slab(ATT_WIDTH), slab(LANES), _const_spec((LANES, ATT_WIDTH)),
                  _const_spec((SSD_WIDTH + ATT_WIDTH, d))],
        out_specs=row(d), out_shape=jax.ShapeDtypeStruct((t, d), F32),
        scratch_shapes=[pltpu.VMEM((TM, LANES), F32),
                        pltpu.VMEM((ATT_WIDTH // LANES, TM, LANES), F32)],
        compiler_params=_params(1), name="outproj")(
            x, y_ssd, o1, l1, o4, l4, o16, l16, ehead, w)


def _head_expand(width_per_head):
    r = jnp.arange(LANES)[:, None]
    c = jnp.arange(SSD_HEADS * width_per_head)[None, :]
    return ((r < HEAD_REP * SSD_HEADS) & (r % SSD_HEADS == c // width_per_head)).astype(BF16)


def _rep_heads(v):
    return jnp.pad(jnp.tile(v.astype(F32), HEAD_REP), (0, LANES - HEAD_REP * SSD_HEADS))[None, :]


def kernel(x, ffn1_norm, ffn1_w_gate, ffn1_w_up, ffn1_w_down, mix_norm, w_in, conv_w, conv_b,
           dt_bias, a_log, d_skip, ssd_norm, q_norm, k_norm, w_out, ffn2_norm, ffn2_w_gate,
           ffn2_w_up, ffn2_w_down):
    batch, seq, d = x.shape
    depth = w_in.shape[0]
    assert d == D_MODEL and seq == RES * BAND_BLOCK
    assert all(wd // dl == BAND_BLOCK for wd, dl in ATT_BRANCHES)
    assert seq % TM == 0 and seq % ATT_ROWS == 0 and seq % SSD_ROWS == 0
    t = batch * seq

    tril = jnp.tril(jnp.ones((SSD_CHUNK, SSD_CHUNK), BF16))
    eye = jnp.eye(SSD_CHUNK, dtype=BF16)
    ehead = _head_expand(SSD_HEAD_DIM)
    masked = jnp.where(tril > 0, 0.0, MASKED).astype(BF16)
    ecol = jnp.concatenate([_head_expand(SSD_CHUNK), jnp.tile(masked, (1, SSD_HEADS))], axis=0)
    half = jnp.arange(LANES) // ATT_HEAD_DIM
    bd = (half[:, None] == half[None, :]).astype(F32) / ATT_HEAD_DIM
    bd2 = jnp.concatenate([bd, bd], axis=0).astype(BF16)
    scale = LOG2E / math.sqrt(ATT_HEAD_DIM)

    xf = x.reshape(t, d)
    for i in range(depth):
        dt_cols = jnp.pad(jnp.tile(w_in[i][:, COL_DT:COL_DT + SSD_HEADS], (1, HEAD_REP)),
                          ((0, 0), (0, LANES - HEAD_REP * SSD_HEADS)))
        w_ssd = jnp.concatenate([w_in[i][:, :COL_DT], dt_cols], axis=1).astype(BF16)
        w_att = w_in[i][:, COL_DT + SSD_HEADS:].astype(BF16)
        qg = jnp.tile(q_norm[i].astype(F32) * scale, LANES // ATT_HEAD_DIM)[None, :]
        kg = jnp.tile(k_norm[i].astype(F32), LANES // ATT_HEAD_DIM)[None, :]

        xf = _ffn(xf, ffn1_norm[i][None, :], ffn1_w_gate[i].astype(BF16),
                  ffn1_w_up[i].astype(BF16), ffn1_w_down[i].astype(BF16))
        z, xs, bc, dt = _proj_ssd(xf, mix_norm[i][None, :], w_ssd, conv_w[i], conv_b[i][None, :], seq)
        qn, kn, vn, q16, k16, v16 = _proj_att(xf, mix_norm[i][None, :], w_att, qg, kg, bd2,
                                              batch, seq)
        y_ssd = _ssd(xs, bc, dt, z, _rep_heads(dt_bias[i]), _rep_heads(a_log[i]),
                     jnp.repeat(d_skip[i].astype(F32), SSD_HEAD_DIM)[None, :],
                     ssd_norm[i][None, :], tril, eye, ehead, ecol, batch, seq)
        o1, l1 = _attn_d1(qn, kn, vn, batch, seq)
        o4, l4 = _attn_d4(q16, k16, v16, batch, seq)
        o16, l16 = _attn_d16(q16, k16, v16, batch, seq)
        xf = _outproj(xf, y_ssd, o1, l1, o4, l4, o16, l16, ehead, w_out[i].astype(BF16), seq)
        xf = _ffn(xf, ffn2_norm[i][None, :], ffn2_w_gate[i].astype(BF16),
                  ffn2_w_up[i].astype(BF16), ffn2_w_down[i].astype(BF16))
    return xf.reshape(batch, seq, d)
```

```python
import functools
import math

import numpy as np
import jax
import jax.numpy as jnp
from jax import lax
from jax.experimental import pallas as pl
from jax.experimental.pallas import tpu as pltpu

F32 = jnp.float32
BF16 = jnp.bfloat16

D_MODEL = 1024
SSD_HEADS = 16
SSD_HEAD_DIM = 64
SSD_WIDTH = SSD_HEADS * SSD_HEAD_DIM
SSD_GROUPS = 4
SSD_STATE = 128
CONV_K = 4
SSD_CHUNK = 128
BC_WIDTH = 2 * SSD_GROUPS * SSD_STATE
CONV_DIM = SSD_WIDTH + BC_WIDTH
ATT_HEADS = 16
ATT_HEAD_DIM = 64
ATT_WIDTH = ATT_HEADS * ATT_HEAD_DIM
ATT_BRANCHES = ((128, 1), (512, 4), (2048, 16))
BAND_BLOCK = 128
EPS = 1e-6

LANES = 128
SUBLANES = 8
MXU_COLS = 256
PROJ_COLS = 256
VMEM_LIMIT_BYTES = 56 * 1024 * 1024

LOG2E = math.log2(math.e)
LN2 = math.log(2.0)
MASKED = -1e30
DEN_ROWS = 16
SCORE_LOOKAHEAD = 6
SCORE_LOOKAHEAD_D1 = 10
DENSE_LOOKAHEAD = 2

HEAD_REP = 3
RES = 16
N_PAIRS = ATT_HEADS // 2

COL_Z = 0
COL_XBC = COL_Z + SSD_WIDTH
COL_DT = COL_XBC + CONV_DIM
SSD_COLS = COL_DT + LANES

TM = 1024
ATT_ROWS = 2048
SSD_ROWS = 1024

def _const_spec(shape):
    nd = len(shape)
    return pl.BlockSpec(shape, lambda *_: (0,) * nd, pipeline_mode=pl.Buffered(1))


def _params(n_axes):
    return pltpu.CompilerParams(dimension_semantics=("arbitrary",) * n_axes,
                                vmem_limit_bytes=VMEM_LIMIT_BYTES)


def _rms(x, gain):
    ms = jnp.mean(x * x, axis=-1, keepdims=True)
    return x * lax.rsqrt(ms + EPS) * gain


def _silu(x):
    h = 0.5 * x
    return h + h * jnp.tanh(h)


def _dot(a, b):
    return jnp.dot(a, b, preferred_element_type=F32)


def _dot_nt(a, b):
    return lax.dot_general(a, b, (((1,), (1,)), ((), ())), preferred_element_type=F32)


def _dot_tn(a, b):
    return lax.dot_general(a, b, (((0,), (0,)), ((), ())), preferred_element_type=F32)


def _split3(v):
    hi = v.astype(BF16)
    r1 = v - hi.astype(F32)
    mid = r1.astype(BF16)
    lo = (r1 - mid.astype(F32)).astype(BF16)
    return hi, mid, lo


def _split3_lanes(v):
    hi, mid, lo = _split3(v)
    lane = lax.broadcasted_iota(jnp.int32, v.shape, 1)
    zero = jnp.zeros_like(hi)
    return jnp.where(lane < SSD_HEADS, hi,
                     jnp.where(lane < 2 * SSD_HEADS, mid,
                               jnp.where(lane < 3 * SSD_HEADS, lo, zero)))


def _ffn_body(x_ref, g_ref, wg_ref, wu_ref, wd_ref, o_ref):
    x = x_ref[...]
    xn = _rms(x, g_ref[...]).astype(BF16)
    gate = _dot(xn, wg_ref[...])
    up = _dot(xn, wu_ref[...])
    h = (_silu(gate) * up).astype(BF16)
    o_ref[...] = x + 0.5 * _dot(h, wd_ref[...])


def _ffn(x, gain, wg, wu, wd):
    t, d = x.shape
    f = wg.shape[1]
    row = pl.BlockSpec((TM, d), lambda i: (i, 0))
    return pl.pallas_call(
        _ffn_body, grid=(t // TM,),
        in_specs=[row, _const_spec((1, d)), _const_spec((d, f)), _const_spec((d, f)),
                  _const_spec((f, d))],
        out_specs=row, out_shape=jax.ShapeDtypeStruct((t, d), F32),
        compiler_params=_params(1), name="ffn")(x, gain, wg, wu, wd)


def _proj_ssd_body(x_ref, g_ref, w_ref, cw_ref, cb_ref, z_ref, xs_ref, bc_ref, dt_ref, cbuf_ref,
                   xn_ref, *, tiles_per_seq):
    tm = x_ref.shape[0]
    n_chunks = CONV_DIM // PROJ_COLS
    per = PROJ_COLS // LANES
    xn_ref[...] = _rms(x_ref[...], g_ref[...]).astype(BF16)

    @pl.when(pl.program_id(0) % tiles_per_seq == 0)
    def _():
        cbuf_ref[:, 0:SUBLANES, :] = jnp.zeros((CONV_DIM // LANES, SUBLANES, LANES), F32)

    main = lambda c: _dot(
        xn_ref[...], w_ref[:, COL_XBC + c * PROJ_COLS:COL_XBC + (c + 1) * PROJ_COLS])
    z_ref[...] = _dot(xn_ref[...], w_ref[:, COL_Z:COL_XBC]).astype(BF16)
    dt_ref[...] = _dot(xn_ref[...], w_ref[:, COL_DT:SSD_COLS])
    pending = [main(c) for c in range(DENSE_LOOKAHEAD)]
    for c in range(n_chunks):
        res = pending.pop(0)
        if c + DENSE_LOOKAHEAD < n_chunks:
            pending.append(main(c + DENSE_LOOKAHEAD))
        for s in range(per):
            slab = c * per + s
            lanes = slice(slab * LANES, (slab + 1) * LANES)
            cbuf_ref[slab, SUBLANES:SUBLANES + tm, :] = res[:, s * LANES:(s + 1) * LANES]
            conv = cb_ref[:, lanes]
            for k in range(CONV_K):
                off = SUBLANES - (CONV_K - 1) + k
                conv = conv + cw_ref[k:k + 1, lanes] * cbuf_ref[slab, off:off + tm, :]
            cbuf_ref[slab, 0:SUBLANES, :] = cbuf_ref[slab, tm:tm + SUBLANES, :]
            act = _silu(conv)
            if slab * LANES < SSD_WIDTH:
                xs_ref[:, lanes] = act
            else:
                bc_ref[:, slab * LANES - SSD_WIDTH:(slab + 1) * LANES - SSD_WIDTH] = act.astype(BF16)


def _proj_ssd(x, gain, w, cw, cb, seq):
    t, d = x.shape
    row = lambda n: pl.BlockSpec((TM, n), lambda i: (i, 0))
    shp = lambda n, dt: jax.ShapeDtypeStruct((t, n), dt)
    return pl.pallas_call(
        functools.partial(_proj_ssd_body, tiles_per_seq=seq // TM), grid=(t // TM,),
        in_specs=[row(d), _const_spec((1, d)), _const_spec((d, SSD_COLS)),
                  _const_spec((CONV_K, CONV_DIM)), _const_spec((1, CONV_DIM))],
        out_specs=[row(SSD_WIDTH), row(SSD_WIDTH), row(BC_WIDTH), row(LANES)],
        out_shape=[shp(SSD_WIDTH, BF16), shp(SSD_WIDTH, F32), shp(BC_WIDTH, BF16),
                   shp(LANES, F32)],
        scratch_shapes=[pltpu.VMEM((CONV_DIM // LANES, SUBLANES + TM, LANES), F32),
                        pltpu.VMEM((TM, d), BF16)],
        compiler_params=_params(1), name="proj_ssd")(x, gain, w, cw, cb)


def _proj_att_body(x_ref, g_ref, w_ref, qg_ref, kg_ref, bd2_ref,
                   qn_ref, kn_ref, vn_ref, q16_ref, k16_ref, v16_ref, scr_ref, xn_ref):
    tm = x_ref.shape[0]
    ni = tm // RES
    per = PROJ_COLS // LANES
    n_slabs = scr_ref.shape[0]
    xn_ref[...] = _rms(x_ref[...], g_ref[...]).astype(BF16)
    bd2 = bd2_ref[...]
    plan = ((qg_ref, qn_ref, q16_ref), (kg_ref, kn_ref, k16_ref), (None, vn_ref, v16_ref))
    chunks = [(t, c) for t in range(3) for c in range(ATT_WIDTH // PROJ_COLS)]
    main = lambda t, c: _dot(
        xn_ref[...], w_ref[:, t * ATT_WIDTH + c * PROJ_COLS:t * ATT_WIDTH + (c + 1) * PROJ_COLS])
    pending = [main(*ch) for ch in chunks[:DENSE_LOOKAHEAD]]
    for ci, (t, c) in enumerate(chunks):
        gain_ref, nat_ref, l16_ref = plan[t]
        res = pending.pop(0)
        if ci + DENSE_LOOKAHEAD < len(chunks):
            pending.append(main(*chunks[ci + DENSE_LOOKAHEAD]))
        for s in range(per):
            lanes = slice(c * PROJ_COLS + s * LANES, c * PROJ_COLS + (s + 1) * LANES)
            y = res[:, s * LANES:(s + 1) * LANES]
            if gain_ref is not None:
                sq = y * y
                hi = sq.astype(BF16)
                lo = (sq - hi.astype(F32)).astype(BF16)
                ms = _dot(jnp.concatenate([hi, lo], axis=1), bd2)
                y = y * lax.rsqrt(ms + EPS) * gain_ref[...]
            nat_ref[:, lanes] = y.astype(BF16)
            slab = (ci * per + s) % n_slabs
            scr_ref[slab] = y
            for r in range(RES):
                l16_ref[r, :, lanes] = scr_ref[slab, pl.ds(r, ni, stride=RES), :].astype(BF16)


def _proj_att(x, gain, w, qg, kg, bd2, batch, seq):
    t, d = x.shape
    tps = seq // TM
    ni = TM // RES
    row = lambda n: pl.BlockSpec((TM, n), lambda i: (i, 0))
    l16 = pl.BlockSpec((None, RES, ni, ATT_WIDTH), lambda i: (i // tps, 0, i % tps, 0))
    nat_shape = jax.ShapeDtypeStruct((t, ATT_WIDTH), BF16)
    l16_shape = jax.ShapeDtypeStruct((batch, RES, seq // RES, ATT_WIDTH), BF16)
    return pl.pallas_call(
        _proj_att_body, grid=(t // TM,),
        in_specs=[row(d), _const_spec((1, d)), _const_spec((d, 3 * ATT_WIDTH)),
                  _const_spec((1, LANES)), _const_spec((1, LANES)),
                  _const_spec((2 * LANES, LANES))],
        out_specs=[row(ATT_WIDTH)] * 3 + [l16] * 3,
        out_shape=[nat_shape] * 3 + [l16_shape] * 3,
        scratch_shapes=[pltpu.VMEM((2 * PROJ_COLS // LANES, TM, LANES), F32),
                        pltpu.VMEM((TM, d), BF16)],
        compiler_params=_params(1), name="proj_att")(x, gain, w, qg, kg, bd2)


def _ssd_body(xs_ref, bc_ref, dt_ref, z_ref, dtb_ref, alog_ref, dskip_ref, nw_ref,
              tril_ref, eye_ref, ehead_ref, ecol_ref, y_ref, state_ref, acumt_ref, *, chunks):
    L = SSD_CHUNK
    gsz = SSD_GROUPS * SSD_STATE

    @pl.when(pl.program_id(1) == 0)
    def _():
        state_ref[...] = jnp.zeros_like(state_ref)

    lane = lax.broadcasted_iota(jnp.int32, (L, SSD_WIDTH), 1)
    even_head = (lane // SSD_HEAD_DIM) % 2 == 0
    heads_per_group = SSD_HEADS // SSD_GROUPS
    gw = heads_per_group * SSD_HEAD_DIM
    groups = range(SSD_GROUPS)

    def front(ci):
        rows = slice(ci * L, (ci + 1) * L)
        cgs = [bc_ref[rows, gsz + g * SSD_STATE:gsz + (g + 1) * SSD_STATE] for g in groups]
        bgs = [bc_ref[rows, g * SSD_STATE:(g + 1) * SSD_STATE] for g in groups]
        cbms = [_dot_nt(cgs[g], bgs[g]) for g in groups]
        dt = jax.nn.softplus(dt_ref[rows, :] + dtb_ref[...])
        adt = dt * (-jnp.exp(alog_ref[...]) * LOG2E)
        tril = tril_ref[...]
        hi, mid, lo = _split3(adt)
        acum = _dot(tril, hi) + _dot(tril, mid) + _dot(tril, lo)
        acumt_ref[ci] = acum.T
        dt_x = _dot(_split3_lanes(dt), ehead_ref[...])
        acum3 = _split3_lanes(acum)
        acum_x = _dot(acum3, ehead_ref[...])
        acum_col = _dot(jnp.concatenate([acum3, eye_ref[...]], axis=1), ecol_ref[...])
        return cgs, bgs, cbms, dt_x, acum_x, acum_col

    def back(ci, cgs, bgs, cbms, dt_x, acum_x, acum_col):
        rows = slice(ci * L, (ci + 1) * L)
        xs = xs_ref[rows, :]
        sts = [state_ref[:, g * gw:(g + 1) * gw] for g in groups]
        y_offs = [_dot(cgs[g], sts[g].astype(BF16)) for g in groups]
        xdt = xs * dt_x
        eacum_x = jnp.exp2(acum_x)
        tot_x = acum_x[L - 1:L, :]
        xdt_b = xdt.astype(BF16)
        zero = jnp.zeros_like(xdt_b)
        xdt_even = jnp.where(even_head, xdt_b, zero)
        xdt_odd = jnp.where(even_head, zero, xdt_b)
        xdec_b = (xdt * jnp.exp2(tot_x - acum_x)).astype(BF16)
        chunk_decay = eacum_x[L - 1:L, :]
        for g in groups:
            gs = slice(g * gw, (g + 1) * gw)
            new_states = _dot_tn(bgs[g], xdec_b[:, gs])
            state_ref[:, gs] = sts[g] * chunk_decay[:, gs] + new_states
        y_parts = []
        for g in groups:
            gs = slice(g * gw, (g + 1) * gw)
            ms = []
            for r in range(heads_per_group):
                h = g * heads_per_group + r
                seg = acum_col[:, h * L:(h + 1) * L] - acumt_ref[ci, h:h + 1, :]
                ms.append((cbms[g] * jnp.exp2(seg)).astype(BF16))
            yd = []
            for pr in range(heads_per_group // 2):
                pair = g * (heads_per_group // 2) + pr
                sl = slice(pair * LANES, (pair + 1) * LANES)
                lhs = jnp.concatenate([ms[2 * pr], ms[2 * pr + 1]], axis=1)
                rhs = jnp.concatenate([xdt_even[:, sl], xdt_odd[:, sl]], axis=0)
                yd.append(_dot(lhs, rhs))
            y_parts.append(jnp.concatenate(yd, axis=1) + y_offs[g] * eacum_x[:, gs])

        y = jnp.concatenate(y_parts, axis=1) + xs * dskip_ref[...]
        y = y * _silu(z_ref[rows, :]).astype(F32)
        outs = []
        for g in groups:
            yg = y[:, g * gw:(g + 1) * gw]
            msq = jnp.mean(yg * yg, axis=-1, keepdims=True)
            outs.append(yg * lax.rsqrt(msq + EPS))
        y_ref[rows, :] = (jnp.concatenate(outs, axis=1) * nw_ref[...]).astype(BF16)

    pending = front(0)
    for ci in range(chunks):
        ready = pending
        if ci + 1 < chunks:
            pending = front(ci + 1)
        back(ci, *ready)


def _ssd(xs, bc, dt, z, dtb, alog, dskip_x, nw, tril, eye, ehead, ecol, batch, seq):
    L = SSD_CHUNK
    ns = seq // SSD_ROWS
    chunks = SSD_ROWS // L
    blk = lambda n: pl.BlockSpec((SSD_ROWS, n), lambda b, s: (b * ns + s, 0))
    return pl.pallas_call(
        functools.partial(_ssd_body, chunks=chunks), grid=(batch, ns),
        in_specs=[blk(SSD_WIDTH), blk(BC_WIDTH), blk(LANES), blk(SSD_WIDTH),
                  _const_spec((1, LANES)), _const_spec((1, LANES)),
                  _const_spec((1, SSD_WIDTH)), _const_spec((1, SSD_WIDTH)),
                  _const_spec((L, L)), _const_spec((L, L)), _const_spec((LANES, SSD_WIDTH)),
                  _const_spec((LANES + L, SSD_HEADS * L))],
        out_specs=blk(SSD_WIDTH),
        out_shape=jax.ShapeDtypeStruct((batch * seq, SSD_WIDTH), BF16),
        scratch_shapes=[pltpu.VMEM((SSD_STATE, SSD_WIDTH), F32),
                        pltpu.VMEM((chunks, L, LANES), F32)],
        compiler_params=_params(2), name="ssd")(
            xs, bc, dt, z, dtb, alog, dskip_x, nw, tril, eye, ehead, ecol)


def _attn_units(units, stat_ref, st_ref):
    blk = BAND_BLOCK
    lane = lax.broadcasted_iota(jnp.int32, (blk, LANES), 1)
    first_head = lane < ATT_HEAD_DIM
    srow = lax.broadcasted_iota(jnp.int32, (LANES, blk), 0)
    first_rows = srow < ATT_HEAD_DIM
    stat_ref[...] = jnp.zeros_like(stat_ref)

    def scores(unit):
        q2 = unit["q"]()
        zero = jnp.zeros_like(q2)
        qst = jnp.concatenate([jnp.where(first_head, q2, zero),
                               jnp.where(first_head, zero, q2)], axis=0)
        return _dot_nt(unit["k"](), qst)

    n_slots = st_ref.shape[0]
    lookahead = n_slots - 1

    def issue(idx):
        st_ref[idx % n_slots] = scores(units[idx])

    for idx in range(min(lookahead, len(units))):
        issue(idx)
    for idx, unit in enumerate(units):
        if idx + lookahead < len(units):
            issue(idx + lookahead)
        bias = unit["bias"]()
        ps, mxs = [], []
        for hh in range(2):
            s = st_ref[idx % n_slots, :, hh * blk:(hh + 1) * blk] + bias
            mx = jnp.max(s, axis=0, keepdims=True)
            ps.append(jnp.exp2(s - mx).astype(BF16))
            mxs.append(mx)
        vt = unit["vt"]()
        ones = jnp.ones((DEN_ROWS, vt.shape[1]), BF16)
        ot = _dot(jnp.concatenate([vt, ones], axis=0), jnp.concatenate(ps, axis=1))
        outs = []
        for hh in range(2):
            den = ot[LANES:LANES + 1, hh * blk:(hh + 1) * blk]
            outs.append(ot[0:LANES, hh * blk:(hh + 1) * blk] * (1.0 / den))
            lse = (mxs[hh] + jnp.log2(den)) * LN2
            for rep in range(HEAD_REP):
                row = rep * ATT_HEADS + 2 * unit["pair"] + hh
                stat_ref[unit["slot"], row:row + 1, :] = lse
        o_t = jnp.where(first_rows, outs[0], outs[1])
        unit["write"](o_t.T.astype(BF16))


def _attn_d1_body(q_ref, kc_ref, kp_ref, vc_ref, vp_ref, bias_ref, o_ref, lse_ref, stat_ref,
                  st_ref):
    blk = BAND_BLOCK
    n_sub = ATT_ROWS // blk
    has_prev = jnp.where(pl.program_id(1) > 0, 1, 0)
    units = []
    for p in range(N_PAIRS):
        sl = slice(p * LANES, (p + 1) * LANES)
        vts = {}

        def vt_block(m, sl=sl, vts=vts):
            if m not in vts:
                v = vp_ref[:, sl] if m == 0 else vc_ref[(m - 1) * blk:m * blk, sl]
                vts[m] = v.T
            return vts[m]

        for j in range(n_sub):
            def write(o, j=j, sl=sl):
                o_ref[j * blk:(j + 1) * blk, sl] = o

            if j == 0:
                k_of = lambda sl=sl: jnp.concatenate([kp_ref[:, sl], kc_ref[0:blk, sl]], axis=0)
                bias_of = lambda: bias_ref[has_prev]
            else:
                k_of = lambda j=j, sl=sl: kc_ref[(j - 1) * blk:(j + 1) * blk, sl]
                bias_of = lambda: bias_ref[1]
            units.append(dict(
                q=lambda j=j, sl=sl: q_ref[j * blk:(j + 1) * blk, sl], k=k_of,
                vt=lambda j=j, f=vt_block: jnp.concatenate([f(j), f(j + 1)], axis=1),
                bias=bias_of, slot=j, pair=p, write=write))
    _attn_units(units, stat_ref, st_ref)
    for j in range(n_sub):
        lse_ref[j * blk:(j + 1) * blk, :] = stat_ref[j].T


def _attn_d4_body(q_ref, kc_ref, kp_ref, vc_ref, vp_ref, bias_ref, o_ref, lse_ref, stat_ref,
                  st_ref):
    blk = BAND_BLOCK
    ni = blk // 4
    n_groups = q_ref.shape[1] // ni
    has_prev = jnp.where(pl.program_id(1) > 0, 1, 0)
    rows = lambda g: slice(g * ni, (g + 1) * ni)
    gather = lambda ref, r, g, sl: [ref[4 * a + r, rows(g), sl] for a in range(4)]

    def window(cur_ref, prev_ref, r, g, sl):
        before = gather(prev_ref, r, 0, sl) if g == 0 else gather(cur_ref, r, g - 1, sl)
        return jnp.concatenate(before, axis=0), jnp.concatenate(gather(cur_ref, r, g, sl), axis=0)

    units = []
    for p in range(N_PAIRS):
        sl = slice(p * LANES, (p + 1) * LANES)
        for g in range(n_groups):
            for r in range(4):
                def write(o, r=r, g=g, sl=sl):
                    for a in range(4):
                        o_ref[4 * a + r, rows(g), sl] = o[a * ni:(a + 1) * ni, :]

                units.append(dict(
                    q=lambda r=r, g=g, sl=sl: jnp.concatenate(gather(q_ref, r, g, sl), axis=0),
                    k=lambda r=r, g=g, sl=sl: jnp.concatenate(
                        window(kc_ref, kp_ref, r, g, sl), axis=0),
                    vt=lambda r=r, g=g, sl=sl: jnp.concatenate(
                        [v.T for v in window(vc_ref, vp_ref, r, g, sl)], axis=1),
                    bias=(lambda: bias_ref[has_prev]) if g == 0 else (lambda: bias_ref[1]),
                    slot=g * 4 + r, pair=p, write=write))
    _attn_units(units, stat_ref, st_ref)
    for g in range(n_groups):
        for r in range(4):
            lt = stat_ref[g * 4 + r].T
            for a in range(4):
                lse_ref[4 * a + r, rows(g), :] = lt[a * ni:(a + 1) * ni, :]


def _attn_d16_body(q_ref, k_ref, v_ref, bias_ref, o_ref, lse_ref, stat_ref, st_ref):
    n_sub = q_ref.shape[0]
    units = []
    for p in range(N_PAIRS):
        sl = slice(p * LANES, (p + 1) * LANES)
        for j in range(n_sub):
            def write(o, j=j, sl=sl):
                o_ref[j, :, sl] = o

            units.append(dict(
                q=lambda j=j, sl=sl: q_ref[j, :, sl], k=lambda j=j, sl=sl: k_ref[j, :, sl],
                vt=lambda j=j, sl=sl: v_ref[j, :, sl].T,
                bias=lambda: bias_ref[0], slot=j, pair=p, write=write))
    _attn_units(units, stat_ref, st_ref)
    for j in range(n_sub):
        lse_ref[j] = stat_ref[j].T


def _attn_scratch(n_sub, nk, lookahead=SCORE_LOOKAHEAD):
    return [pltpu.VMEM((n_sub, LANES, BAND_BLOCK), F32),
            pltpu.VMEM((lookahead + 1, nk, 2 * BAND_BLOCK), F32)]


def _band_bias(kk, qi, need_prev):
    ok = (kk[:, None] >= qi[None, :]) & (kk[:, None] <= qi[None, :] + BAND_BLOCK)
    if need_prev is not None:
        ok = ok & need_prev[:, None]
    return np.where(ok, 0.0, -np.inf).astype(np.float32)


def _attn_d1(q, k, v, batch, seq):
    blk = BAND_BLOCK
    w = ATT_WIDTH
    nt = seq // ATT_ROWS
    per = ATT_ROWS // blk
    kk = np.arange(2 * blk)
    qi = np.arange(blk)
    bias = jnp.asarray(np.stack([_band_bias(kk, qi, kk >= blk), _band_bias(kk, qi, None)]))
    cur = lambda n: pl.BlockSpec((ATT_ROWS, n), lambda b, i: (b * nt + i, 0))
    prev = pl.BlockSpec((blk, w), lambda b, i: (b * nt * per + jnp.maximum(i * per - 1, 0), 0))
    return pl.pallas_call(
        _attn_d1_body, grid=(batch, nt),
        in_specs=[cur(w), cur(w), prev, cur(w), prev, _const_spec(bias.shape)],
        out_specs=[cur(w), cur(LANES)],
        out_shape=[jax.ShapeDtypeStruct((batch * seq, w), BF16),
                   jax.ShapeDtypeStruct((batch * seq, LANES), F32)],
        scratch_shapes=_attn_scratch(per, 2 * blk, SCORE_LOOKAHEAD_D1),
        compiler_params=_params(2), name="attn_d1")(q, k, k, v, v, bias)


def _attn_d4(q16, k16, v16, batch, seq):
    blk = BAND_BLOCK
    w = ATT_WIDTH
    ni = blk // 4
    nb = seq // 4 // blk
    pos = np.arange(blk)
    true_i = 4 * (pos % ni) + pos // ni
    kk = np.concatenate([true_i, true_i + blk])
    bias = jnp.asarray(np.stack([_band_bias(kk, true_i, kk >= blk), _band_bias(kk, true_i, None)]))
    per = ATT_ROWS // (4 * blk)
    cur = lambda n: pl.BlockSpec((None, RES, per * ni, n), lambda b, i: (b, 0, i, 0))
    prev = pl.BlockSpec((None, RES, ni, w), lambda b, i: (b, 0, jnp.maximum(i * per - 1, 0), 0))
    return pl.pallas_call(
        _attn_d4_body, grid=(batch, nb // per),
        in_specs=[cur(w), cur(w), prev, cur(w), prev, _const_spec(bias.shape)],
        out_specs=[cur(w), cur(LANES)],
        out_shape=[jax.ShapeDtypeStruct((batch, RES, seq // RES, w), BF16),
                   jax.ShapeDtypeStruct((batch, RES, seq // RES, LANES), F32)],
        scratch_shapes=_attn_scratch(4 * per, 2 * blk),
        compiler_params=_params(2), name="attn_d4")(q16, k16, k16, v16, v16, bias)


def _attn_d16(q16, k16, v16, batch, seq):
    blk = BAND_BLOCK
    w = ATT_WIDTH
    n_sub = ATT_ROWS // blk
    assert seq // RES == blk and RES % n_sub == 0
    kk = np.arange(blk)
    bias = jnp.asarray(_band_bias(kk + blk, kk, None)[None])
    spec = lambda n: pl.BlockSpec((None, n_sub, blk, n), lambda b, g: (b, g, 0, 0))
    return pl.pallas_call(
        _attn_d16_body, grid=(batch, RES // n_sub),
        in_specs=[spec(w), spec(w), spec(w), _const_spec(bias.shape)],
        out_specs=[spec(w), spec(LANES)],
        out_shape=[jax.ShapeDtypeStruct((batch, RES, blk, w), BF16),
                   jax.ShapeDtypeStruct((batch, RES, blk, LANES), F32)],
        scratch_shapes=_attn_scratch(n_sub, blk),
        compiler_params=_params(2), name="attn_d16")(q16, k16, v16, bias)


def _outproj_body(x_ref, ys_ref, o1_ref, l1_ref, o4_ref, l4_ref, o16_ref, l16_ref, ehead_ref,
                  w_ref, out_ref, small_ref, perm_ref):
    tm = x_ref.shape[0]
    ni = tm // RES

    def to_natural(l_ref):
        for r in range(RES):
            small_ref[pl.ds(r, ni, stride=RES), :] = l_ref[r]
        return small_ref[...]

    def to_l16(val):
        small_ref[...] = val
        return jnp.concatenate([small_ref[pl.ds(r, ni, stride=RES), :] for r in range(RES)], axis=0)

    l1 = l1_ref[...]
    l4 = to_natural(l4_ref)
    l16 = to_natural(l16_ref)
    m = jnp.maximum(jnp.maximum(l1, l4), l16)
    e1, e4, e16 = jnp.exp(l1 - m), jnp.exp(l4 - m), jnp.exp(l16 - m)
    inv = 1.0 / (e1 + e4 + e16)
    a1 = _split3_lanes(e1 * inv)
    a4 = _split3_lanes(to_l16(e4 * inv))
    a16 = _split3_lanes(to_l16(e16 * inv))
    acc = x_ref[...] + _dot(ys_ref[...], w_ref[0:SSD_WIDTH, :])
    per = MXU_COLS // LANES
    for c in range(ATT_WIDTH // MXU_COLS):
        cols = slice(c * MXU_COLS, (c + 1) * MXU_COLS)
        ehead = ehead_ref[:, cols]
        o4 = o4_ref[:, :, cols].reshape(tm, MXU_COLS).astype(F32)
        o16 = o16_ref[:, :, cols].reshape(tm, MXU_COLS).astype(F32)
        y_l16 = _dot(a4, ehead) * o4 + _dot(a16, ehead) * o16
        for r in range(RES):
            for s in range(per):
                perm_ref[c * per + s, pl.ds(r, ni, stride=RES), :] = (
                    y_l16[r * ni:(r + 1) * ni, s * LANES:(s + 1) * LANES])
        y_nat = jnp.concatenate([perm_ref[c * per + s] for s in range(per)], axis=1)
        y_att = _dot(a1, ehead) * o1_ref[:, cols].astype(F32) + y_nat
        acc = acc + _dot(y_att.astype(BF16),
                         w_ref[SSD_WIDTH + c * MXU_COLS:SSD_WIDTH + (c + 1) * MXU_COLS, :])
    out_ref[...] = acc


def _outproj(x, y_ssd, o1, l1, o4, l4, o16, l16, ehead, w, seq):
    t, d = x.shape
    tps = seq // TM
    ni = TM // RES
    row = lambda n: pl.BlockSpec((TM, n), lambda i: (i, 0))
    slab = lambda n: pl.BlockSpec((None, RES, ni, n), lambda i: (i // tps, 0, i % tps, 0))
    return pl.pallas_call(
        _outproj_body, grid=(t // TM,),
        in_specs=[row(d), row(SSD_WIDTH), row(ATT_WIDTH), row(LANES), slab(ATT_WIDTH), slab(LANES),
                  slab(ATT_WIDTH), slab(LANES), _const_spec((LANES, ATT_WIDTH)),
                  _const_spec((SSD_WIDTH + ATT_WIDTH, d))],
        out_specs=row(d), out_shape=jax.ShapeDtypeStruct((t, d), F32),
        scratch_shapes=[pltpu.VMEM((TM, LANES), F32),
                        pltpu.VMEM((ATT_WIDTH // LANES, TM, LANES), F32)],
        compiler_params=_params(1), name="outproj")(
            x, y_ssd, o1, l1, o4, l4, o16, l16, ehead, w)


def _head_expand(width_per_head):
    r = jnp.arange(LANES)[:, None]
    c = jnp.arange(SSD_HEADS * width_per_head)[None, :]
    return ((r < HEAD_REP * SSD_HEADS) & (r % SSD_HEADS == c // width_per_head)).astype(BF16)


def _rep_heads(v):
    return jnp.pad(jnp.tile(v.astype(F32), HEAD_REP), (0, LANES - HEAD_REP * SSD_HEADS))[None, :]


def kernel(x, ffn1_norm, ffn1_w_gate, ffn1_w_up, ffn1_w_down, mix_norm, w_in, conv_w, conv_b,
           dt_bias, a_log, d_skip, ssd_norm, q_norm, k_norm, w_out, ffn2_norm, ffn2_w_gate,
           ffn2_w_up, ffn2_w_down):
    batch, seq, d = x.shape
    depth = w_in.shape[0]
    assert d == D_MODEL and seq == RES * BAND_BLOCK
    assert all(wd // dl == BAND_BLOCK for wd, dl in ATT_BRANCHES)
    assert seq % TM == 0 and seq % ATT_ROWS == 0 and seq % SSD_ROWS == 0
    t = batch * seq

    tril = jnp.tril(jnp.ones((SSD_CHUNK, SSD_CHUNK), BF16))
    eye = jnp.eye(SSD_CHUNK, dtype=BF16)
    ehead = _head_expand(SSD_HEAD_DIM)
    masked = jnp.where(tril > 0, 0.0, MASKED).astype(BF16)
    ecol = jnp.concatenate([_head_expand(SSD_CHUNK), jnp.tile(masked, (1, SSD_HEADS))], axis=0)
    half = jnp.arange(LANES) // ATT_HEAD_DIM
    bd = (half[:, None] == half[None, :]).astype(F32) / ATT_HEAD_DIM
    bd2 = jnp.concatenate([bd, bd], axis=0).astype(BF16)
    scale = LOG2E / math.sqrt(ATT_HEAD_DIM)

    xf = x.reshape(t, d)
    for i in range(depth):
        dt_cols = jnp.pad(jnp.tile(w_in[i][:, COL_DT:COL_DT + SSD_HEADS], (1, HEAD_REP)),
                          ((0, 0), (0, LANES - HEAD_REP * SSD_HEADS)))
        w_ssd = jnp.concatenate([w_in[i][:, :COL_DT], dt_cols], axis=1).astype(BF16)
        w_att = w_in[i][:, COL_DT + SSD_HEADS:].astype(BF16)
        qg = jnp.tile(q_norm[i].astype(F32) * scale, LANES // ATT_HEAD_DIM)[None, :]
        kg = jnp.tile(k_norm[i].astype(F32), LANES // ATT_HEAD_DIM)[None, :]

        xf = _ffn(xf, ffn1_norm[i][None, :], ffn1_w_gate[i].astype(BF16),
                  ffn1_w_up[i].astype(BF16), ffn1_w_down[i].astype(BF16))
        z, xs, bc, dt = _proj_ssd(xf, mix_norm[i][None, :], w_ssd, conv_w[i], conv_b[i][None, :], seq)
        qn, kn, vn, q16, k16, v16 = _proj_att(xf, mix_norm[i][None, :], w_att, qg, kg, bd2,
                                              batch, seq)
        y_ssd = _ssd(xs, bc, dt, z, _rep_heads(dt_bias[i]), _rep_heads(a_log[i]),
                     jnp.repeat(d_skip[i].astype(F32), SSD_HEAD_DIM)[None, :],
                     ssd_norm[i][None, :], tril, eye, ehead, ecol, batch, seq)
        o1, l1 = _attn_d1(qn, kn, vn, batch, seq)
        o4, l4 = _attn_d4(q16, k16, v16, batch, seq)
        o16, l16 = _attn_d16(q16, k16, v16, batch, seq)
        xf = _outproj(xf, y_ssd, o1, l1, o4, l4, o16, l16, ehead, w_out[i].astype(BF16), seq)
        xf = _ffn(xf, ffn2_norm[i][None, :], ffn2_w_gate[i].astype(BF16),
                  ffn2_w_up[i].astype(BF16), ffn2_w_down[i].astype(BF16))
    return xf.reshape(batch, seq, d)
```

```python
import functools
import math

import numpy as np
import jax
import jax.numpy as jnp
from jax import lax
from jax.experimental import pallas as pl
from jax.experimental.pallas import tpu as pltpu

F32 = jnp.float32
BF16 = jnp.bfloat16

D_MODEL = 1024
SSD_HEADS = 16
SSD_HEAD_DIM = 64
SSD_WIDTH = SSD_HEADS * SSD_HEAD_DIM
SSD_GROUPS = 4
SSD_STATE = 128
CONV_K = 4
SSD_CHUNK = 128
BC_WIDTH = 2 * SSD_GROUPS * SSD_STATE
CONV_DIM = SSD_WIDTH + BC_WIDTH
ATT_HEADS = 16
ATT_HEAD_DIM = 64
ATT_WIDTH = ATT_HEADS * ATT_HEAD_DIM
ATT_BRANCHES = ((128, 1), (512, 4), (2048, 16))
BAND_BLOCK = 128
EPS = 1e-6

LANES = 128
SUBLANES = 8
MXU_COLS = 256
PROJ_COLS = 256
VMEM_LIMIT_BYTES = 56 * 1024 * 1024

LOG2E = math.log2(math.e)
LN2 = math.log(2.0)
MASKED = -1e30
DEN_ROWS = 16
SCORE_LOOKAHEAD = 6
SCORE_LOOKAHEAD_D1 = 10
DENSE_LOOKAHEAD = 2

HEAD_REP = 3
RES = 16
N_PAIRS = ATT_HEADS // 2

COL_Z = 0
COL_XBC = COL_Z + SSD_WIDTH
COL_DT = COL_XBC + CONV_DIM
SSD_COLS = COL_DT + LANES

TM = 1024
ATT_ROWS = 2048
SSD_ROWS = 1024

def _const_spec(shape):
    nd = len(shape)
    return pl.BlockSpec(shape, lambda *_: (0,) * nd, pipeline_mode=pl.Buffered(1))


def _params(n_axes):
    return pltpu.CompilerParams(dimension_semantics=("arbitrary",) * n_axes,
                                vmem_limit_bytes=VMEM_LIMIT_BYTES)


def _rms(x, gain):
    ms = jnp.mean(x * x, axis=-1, keepdims=True)
    return x * lax.rsqrt(ms + EPS) * gain


def _silu(x):
    h = 0.5 * x
    return h + h * jnp.tanh(h)


def _dot(a, b):
    return jnp.dot(a, b, preferred_element_type=F32)


def _dot_nt(a, b):
    return lax.dot_general(a, b, (((1,), (1,)), ((), ())), preferred_element_type=F32)


def _dot_tn(a, b):
    return lax.dot_general(a, b, (((0,), (0,)), ((), ())), preferred_element_type=F32)


def _split3(v):
    hi = v.astype(BF16)
    r1 = v - hi.astype(F32)
    mid = r1.astype(BF16)
    lo = (r1 - mid.astype(F32)).astype(BF16)
    return hi, mid, lo


def _split3_lanes(v):
    hi, mid, lo = _split3(v)
    lane = lax.broadcasted_iota(jnp.int32, v.shape, 1)
    zero = jnp.zeros_like(hi)
    return jnp.where(lane < SSD_HEADS, hi,
                     jnp.where(lane < 2 * SSD_HEADS, mid,
                               jnp.where(lane < 3 * SSD_HEADS, lo, zero)))


def _ffn_body(x_ref, g_ref, wg_ref, wu_ref, wd_ref, o_ref):
    x = x_ref[...]
    xn = _rms(x, g_ref[...]).astype(BF16)
    gate = _dot(xn, wg_ref[...])
    up = _dot(xn, wu_ref[...])
    h = (_silu(gate) * up).astype(BF16)
    o_ref[...] = x + 0.5 * _dot(h, wd_ref[...])


def _ffn(x, gain, wg, wu, wd):
    t, d = x.shape
    f = wg.shape[1]
    row = pl.BlockSpec((TM, d), lambda i: (i, 0))
    return pl.pallas_call(
        _ffn_body, grid=(t // TM,),
        in_specs=[row, _const_spec((1, d)), _const_spec((d, f)), _const_spec((d, f)),
                  _const_spec((f, d))],
        out_specs=row, out_shape=jax.ShapeDtypeStruct((t, d), F32),
        compiler_params=_params(1), name="ffn")(x, gain, wg, wu, wd)


def _proj_ssd_body(x_ref, g_ref, w_ref, cw_ref, cb_ref, z_ref, xs_ref, bc_ref, dt_ref, cbuf_ref,
                   xn_ref, *, tiles_per_seq):
    tm = x_ref.shape[0]
    n_chunks = CONV_DIM // PROJ_COLS
    per = PROJ_COLS // LANES
    xn_ref[...] = _rms(x_ref[...], g_ref[...]).astype(BF16)

    @pl.when(pl.program_id(0) % tiles_per_seq == 0)
    def _():
        cbuf_ref[:, 0:SUBLANES, :] = jnp.zeros((CONV_DIM // LANES, SUBLANES, LANES), F32)

    main = lambda c: _dot(
        xn_ref[...], w_ref[:, COL_XBC + c * PROJ_COLS:COL_XBC + (c + 1) * PROJ_COLS])
    z_ref[...] = _dot(xn_ref[...], w_ref[:, COL_Z:COL_XBC]).astype(BF16)
    dt_ref[...] = _dot(xn_ref[...], w_ref[:, COL_DT:SSD_COLS])
    pending = [main(c) for c in range(DENSE_LOOKAHEAD)]
    for c in range(n_chunks):
        res = pending.pop(0)
        if c + DENSE_LOOKAHEAD < n_chunks:
            pending.append(main(c + DENSE_LOOKAHEAD))
        for s in range(per):
            slab = c * per + s
            lanes = slice(slab * LANES, (slab + 1) * LANES)
            cbuf_ref[slab, SUBLANES:SUBLANES + tm, :] = res[:, s * LANES:(s + 1) * LANES]
            conv = cb_ref[:, lanes]
            for k in range(CONV_K):
                off = SUBLANES - (CONV_K - 1) + k
                conv = conv + cw_ref[k:k + 1, lanes] * cbuf_ref[slab, off:off + tm, :]
            cbuf_ref[slab, 0:SUBLANES, :] = cbuf_ref[slab, tm:tm + SUBLANES, :]
            act = _silu(conv)
            if slab * LANES < SSD_WIDTH:
                xs_ref[:, lanes] = act
            else:
                bc_ref[:, slab * LANES - SSD_WIDTH:(slab + 1) * LANES - SSD_WIDTH] = act.astype(BF16)


def _proj_ssd(x, gain, w, cw, cb, seq):
    t, d = x.shape
    row = lambda n: pl.BlockSpec((TM, n), lambda i: (i, 0))
    shp = lambda n, dt: jax.ShapeDtypeStruct((t, n), dt)
    return pl.pallas_call(
        functools.partial(_proj_ssd_body, tiles_per_seq=seq // TM), grid=(t // TM,),
        in_specs=[row(d), _const_spec((1, d)), _const_spec((d, SSD_COLS)),
                  _const_spec((CONV_K, CONV_DIM)), _const_spec((1, CONV_DIM))],
        out_specs=[row(SSD_WIDTH), row(SSD_WIDTH), row(BC_WIDTH), row(LANES)],
        out_shape=[shp(SSD_WIDTH, BF16), shp(SSD_WIDTH, F32), shp(BC_WIDTH, BF16),
                   shp(LANES, F32)],
        scratch_shapes=[pltpu.VMEM((CONV_DIM // LANES, SUBLANES + TM, LANES), F32),
                        pltpu.VMEM((TM, d), BF16)],
        compiler_params=_params(1), name="proj_ssd")(x, gain, w, cw, cb)


def _proj_att_body(x_ref, g_ref, w_ref, qg_ref, kg_ref, bd2_ref,
                   qn_ref, kn_ref, vn_ref, q16_ref, k16_ref, v16_ref, scr_ref, scr2_ref, xn_ref):
    tm = x_ref.shape[0]
    ni = tm // RES
    per = PROJ_COLS // LANES
    n_slabs = scr_ref.shape[0]
    xn_ref[...] = _rms(x_ref[...], g_ref[...]).astype(BF16)
    bd2 = bd2_ref[...]
    plan = ((qg_ref, qn_ref, q16_ref), (kg_ref, kn_ref, k16_ref), (None, vn_ref, v16_ref))
    chunks = [(t, c) for t in range(3) for c in range(ATT_WIDTH // PROJ_COLS)]
    main = lambda t, c: _dot(
        xn_ref[...], w_ref[:, t * ATT_WIDTH + c * PROJ_COLS:t * ATT_WIDTH + (c + 1) * PROJ_COLS])
    pending = [main(*ch) for ch in chunks[:DENSE_LOOKAHEAD]]
    for ci, (t, c) in enumerate(chunks):
        gain_ref, nat_ref, l16_ref = plan[t]
        res = pending.pop(0)
        if ci + DENSE_LOOKAHEAD < len(chunks):
            pending.append(main(*chunks[ci + DENSE_LOOKAHEAD]))
        for s in range(per):
            lanes = slice(c * PROJ_COLS + s * LANES, c * PROJ_COLS + (s + 1) * LANES)
            y = res[:, s * LANES:(s + 1) * LANES]
            if gain_ref is not None:
                sq = y * y
                hi = sq.astype(BF16)
                lo = (sq - hi.astype(F32)).astype(BF16)
                ms = _dot(jnp.concatenate([hi, lo], axis=1), bd2)
                y = y * lax.rsqrt(ms + EPS) * gain_ref[...]
            nat_ref[:, lanes] = y.astype(BF16)
            slab = (ci * per + s) % n_slabs
            scr_ref[slab] = y
            nj = tm // 4
            for r in range(4):
                scr2_ref[slab, r * nj:(r + 1) * nj, :] = scr_ref[slab, pl.ds(r, nj, stride=4), :]
            for r in range(4):
                for a in range(4):
                    l16_ref[4 * a + r, :, lanes] = (
                        scr2_ref[slab, pl.ds(r * nj + a, ni, stride=4), :].astype(BF16))


def _proj_att(x, gain, w, qg, kg, bd2, batch, seq):
    t, d = x.shape
    tps = seq // TM
    ni = TM // RES
    row = lambda n: pl.BlockSpec((TM, n), lambda i: (i, 0))
    l16 = pl.BlockSpec((None, RES, ni, ATT_WIDTH), lambda i: (i // tps, 0, i % tps, 0))
    nat_shape = jax.ShapeDtypeStruct((t, ATT_WIDTH), BF16)
    l16_shape = jax.ShapeDtypeStruct((batch, RES, seq // RES, ATT_WIDTH), BF16)
    return pl.pallas_call(
        _proj_att_body, grid=(t // TM,),
        in_specs=[row(d), _const_spec((1, d)), _const_spec((d, 3 * ATT_WIDTH)),
                  _const_spec((1, LANES)), _const_spec((1, LANES)),
                  _const_spec((2 * LANES, LANES))],
        out_specs=[row(ATT_WIDTH)] * 3 + [l16] * 3,
        out_shape=[nat_shape] * 3 + [l16_shape] * 3,
        scratch_shapes=[pltpu.VMEM((2 * PROJ_COLS // LANES, TM, LANES), F32),
                        pltpu.VMEM((2 * PROJ_COLS // LANES, TM, LANES), F32),
                        pltpu.VMEM((TM, d), BF16)],
        compiler_params=_params(1), name="proj_att")(x, gain, w, qg, kg, bd2)


def _ssd_body(xs_ref, bc_ref, dt_ref, z_ref, dtb_ref, alog_ref, dskip_ref, nw_ref,
              tril_ref, eye_ref, ehead_ref, ecol_ref, y_ref, state_ref, acumt_ref, *, chunks):
    L = SSD_CHUNK
    gsz = SSD_GROUPS * SSD_STATE

    @pl.when(pl.program_id(1) == 0)
    def _():
        state_ref[...] = jnp.zeros_like(state_ref)

    lane = lax.broadcasted_iota(jnp.int32, (L, SSD_WIDTH), 1)
    even_head = (lane // SSD_HEAD_DIM) % 2 == 0
    heads_per_group = SSD_HEADS // SSD_GROUPS
    gw = heads_per_group * SSD_HEAD_DIM
    groups = range(SSD_GROUPS)

    def front(ci):
        rows = slice(ci * L, (ci + 1) * L)
        cgs = [bc_ref[rows, gsz + g * SSD_STATE:gsz + (g + 1) * SSD_STATE] for g in groups]
        bgs = [bc_ref[rows, g * SSD_STATE:(g + 1) * SSD_STATE] for g in groups]
        cbms = [_dot_nt(cgs[g], bgs[g]) for g in groups]
        dt = jax.nn.softplus(dt_ref[rows, :] + dtb_ref[...])
        adt = dt * (-jnp.exp(alog_ref[...]) * LOG2E)
        tril = tril_ref[...]
        hi, mid, lo = _split3(adt)
        acum = _dot(tril, hi) + _dot(tril, mid) + _dot(tril, lo)
        acumt_ref[ci] = acum.T
        dt_x = _dot(_split3_lanes(dt), ehead_ref[...])
        acum3 = _split3_lanes(acum)
        acum_x = _dot(acum3, ehead_ref[...])
        acum_col = _dot(jnp.concatenate([acum3, eye_ref[...]], axis=1), ecol_ref[...])
        return cgs, bgs, cbms, dt_x, acum_x, acum_col

    def back(ci, cgs, bgs, cbms, dt_x, acum_x, acum_col):
        rows = slice(ci * L, (ci + 1) * L)
        xs = xs_ref[rows, :]
        sts = [state_ref[:, g * gw:(g + 1) * gw] for g in groups]
        y_offs = [_dot(cgs[g], sts[g].astype(BF16)) for g in groups]
        xdt = xs * dt_x
        eacum_x = jnp.exp2(acum_x)
        tot_x = acum_x[L - 1:L, :]
        xdt_b = xdt.astype(BF16)
        zero = jnp.zeros_like(xdt_b)
        xdt_even = jnp.where(even_head, xdt_b, zero)
        xdt_odd = jnp.where(even_head, zero, xdt_b)
        xdec_b = (xdt * jnp.exp2(tot_x - acum_x)).astype(BF16)
        chunk_decay = eacum_x[L - 1:L, :]
        for g in groups:
            gs = slice(g * gw, (g + 1) * gw)
            new_states = _dot_tn(bgs[g], xdec_b[:, gs])
            state_ref[:, gs] = sts[g] * chunk_decay[:, gs] + new_states
        y_parts = []
        for g in groups:
            gs = slice(g * gw, (g + 1) * gw)
            ms = []
            for r in range(heads_per_group):
                h = g * heads_per_group + r
                seg = acum_col[:, h * L:(h + 1) * L] - acumt_ref[ci, h:h + 1, :]
                ms.append((cbms[g] * jnp.exp2(seg)).astype(BF16))
            yd = []
            for pr in range(heads_per_group // 2):
                pair = g * (heads_per_group // 2) + pr
                sl = slice(pair * LANES, (pair + 1) * LANES)
                lhs = jnp.concatenate([ms[2 * pr], ms[2 * pr + 1]], axis=1)
                rhs = jnp.concatenate([xdt_even[:, sl], xdt_odd[:, sl]], axis=0)
                yd.append(_dot(lhs, rhs))
            y_parts.append(jnp.concatenate(yd, axis=1) + y_offs[g] * eacum_x[:, gs])

        y = jnp.concatenate(y_parts, axis=1) + xs * dskip_ref[...]
        y = y * _silu(z_ref[rows, :]).astype(F32)
        outs = []
        for g in groups:
            yg = y[:, g * gw:(g + 1) * gw]
            msq = jnp.mean(yg * yg, axis=-1, keepdims=True)
            outs.append(yg * lax.rsqrt(msq + EPS))
        y_ref[rows, :] = (jnp.concatenate(outs, axis=1) * nw_ref[...]).astype(BF16)

    pending = front(0)
    for ci in range(chunks):
        ready = pending
        if ci + 1 < chunks:
            pending = front(ci + 1)
        back(ci, *ready)


def _ssd(xs, bc, dt, z, dtb, alog, dskip_x, nw, tril, eye, ehead, ecol, batch, seq):
    L = SSD_CHUNK
    ns = seq // SSD_ROWS
    chunks = SSD_ROWS // L
    blk = lambda n: pl.BlockSpec((SSD_ROWS, n), lambda b, s: (b * ns + s, 0))
    return pl.pallas_call(
        functools.partial(_ssd_body, chunks=chunks), grid=(batch, ns),
        in_specs=[blk(SSD_WIDTH), blk(BC_WIDTH), blk(LANES), blk(SSD_WIDTH),
                  _const_spec((1, LANES)), _const_spec((1, LANES)),
                  _const_spec((1, SSD_WIDTH)), _const_spec((1, SSD_WIDTH)),
                  _const_spec((L, L)), _const_spec((L, L)), _const_spec((LANES, SSD_WIDTH)),
                  _const_spec((LANES + L, SSD_HEADS * L))],
        out_specs=blk(SSD_WIDTH),
        out_shape=jax.ShapeDtypeStruct((batch * seq, SSD_WIDTH), BF16),
        scratch_shapes=[pltpu.VMEM((SSD_STATE, SSD_WIDTH), F32),
                        pltpu.VMEM((chunks, L, LANES), F32)],
        compiler_params=_params(2), name="ssd")(
            xs, bc, dt, z, dtb, alog, dskip_x, nw, tril, eye, ehead, ecol)


def _attn_units(units, stat_ref, st_ref):
    blk = BAND_BLOCK
    lane = lax.broadcasted_iota(jnp.int32, (blk, LANES), 1)
    first_head = lane < ATT_HEAD_DIM
    srow = lax.broadcasted_iota(jnp.int32, (LANES, blk), 0)
    first_rows = srow < ATT_HEAD_DIM
    stat_ref[...] = jnp.zeros_like(stat_ref)

    def scores(unit):
        q2 = unit["q"]()
        zero = jnp.zeros_like(q2)
        qst = jnp.concatenate([jnp.where(first_head, q2, zero),
                               jnp.where(first_head, zero, q2)], axis=0)
        return _dot_nt(unit["k"](), qst)

    n_slots = st_ref.shape[0]
    lookahead = n_slots - 1

    def issue(idx):
        st_ref[idx % n_slots] = scores(units[idx])

    for idx in range(min(lookahead, len(units))):
        issue(idx)
    for idx, unit in enumerate(units):
        if idx + lookahead < len(units):
            issue(idx + lookahead)
        bias = unit["bias"]()
        ps, mxs = [], []
        for hh in range(2):
            s = st_ref[idx % n_slots, :, hh * blk:(hh + 1) * blk] + bias
            mx = jnp.max(s, axis=0, keepdims=True)
            ps.append(jnp.exp2(s - mx).astype(BF16))
            mxs.append(mx)
        vt = unit["vt"]()
        ones = jnp.ones((DEN_ROWS, vt.shape[1]), BF16)
        ot = _dot(jnp.concatenate([vt, ones], axis=0), jnp.concatenate(ps, axis=1))
        outs = []
        for hh in range(2):
            den = ot[LANES:LANES + 1, hh * blk:(hh + 1) * blk]
            outs.append(ot[0:LANES, hh * blk:(hh + 1) * blk] * (1.0 / den))
            lse = (mxs[hh] + jnp.log2(den)) * LN2
            for rep in range(HEAD_REP):
                row = rep * ATT_HEADS + 2 * unit["pair"] + hh
                stat_ref[unit["slot"], row:row + 1, :] = lse
        o_t = jnp.where(first_rows, outs[0], outs[1])
        unit["write"](o_t.T.astype(BF16))


def _attn_d1_body(q_ref, kc_ref, kp_ref, vc_ref, vp_ref, bias_ref, o_ref, lse_ref, stat_ref,
                  st_ref):
    blk = BAND_BLOCK
    n_sub = ATT_ROWS // blk
    has_prev = jnp.where(pl.program_id(1) > 0, 1, 0)
    units = []
    for p in range(N_PAIRS):
        sl = slice(p * LANES, (p + 1) * LANES)
        vts = {}

        def vt_block(m, sl=sl, vts=vts):
            if m not in vts:
                v = vp_ref[:, sl] if m == 0 else vc_ref[(m - 1) * blk:m * blk, sl]
                vts[m] = v.T
            return vts[m]

        for j in range(n_sub):
            def write(o, j=j, sl=sl):
                o_ref[j * blk:(j + 1) * blk, sl] = o

            if j == 0:
                k_of = lambda sl=sl: jnp.concatenate([kp_ref[:, sl], kc_ref[0:blk, sl]], axis=0)
                bias_of = lambda: bias_ref[has_prev]
            else:
                k_of = lambda j=j, sl=sl: kc_ref[(j - 1) * blk:(j + 1) * blk, sl]
                bias_of = lambda: bias_ref[1]
            units.append(dict(
                q=lambda j=j, sl=sl: q_ref[j * blk:(j + 1) * blk, sl], k=k_of,
                vt=lambda j=j, f=vt_block: jnp.concatenate([f(j), f(j + 1)], axis=1),
                bias=bias_of, slot=j, pair=p, write=write))
    _attn_units(units, stat_ref, st_ref)
    for j in range(n_sub):
        lse_ref[j * blk:(j + 1) * blk, :] = stat_ref[j].T


def _attn_d4_body(q_ref, kc_ref, kp_ref, vc_ref, vp_ref, bias_ref, o_ref, lse_ref, stat_ref,
                  st_ref):
    blk = BAND_BLOCK
    ni = blk // 4
    n_groups = q_ref.shape[1] // ni
    has_prev = jnp.where(pl.program_id(1) > 0, 1, 0)
    rows = lambda g: slice(g * ni, (g + 1) * ni)
    gather = lambda ref, r, g, sl: [ref[4 * a + r, rows(g), sl] for a in range(4)]

    def window(cur_ref, prev_ref, r, g, sl):
        before = gather(prev_ref, r, 0, sl) if g == 0 else gather(cur_ref, r, g - 1, sl)
        return jnp.concatenate(before, axis=0), jnp.concatenate(gather(cur_ref, r, g, sl), axis=0)

    units = []
    for p in range(N_PAIRS):
        sl = slice(p * LANES, (p + 1) * LANES)
        for g in range(n_groups):
            for r in range(4):
                def write(o, r=r, g=g, sl=sl):
                    for a in range(4):
                        o_ref[4 * a + r, rows(g), sl] = o[a * ni:(a + 1) * ni, :]

                units.append(dict(
                    q=lambda r=r, g=g, sl=sl: jnp.concatenate(gather(q_ref, r, g, sl), axis=0),
                    k=lambda r=r, g=g, sl=sl: jnp.concatenate(
                        window(kc_ref, kp_ref, r, g, sl), axis=0),
                    vt=lambda r=r, g=g, sl=sl: jnp.concatenate(
                        [v.T for v in window(vc_ref, vp_ref, r, g, sl)], axis=1),
                    bias=(lambda: bias_ref[has_prev]) if g == 0 else (lambda: bias_ref[1]),
                    slot=g * 4 + r, pair=p, write=write))
    _attn_units(units, stat_ref, st_ref)
    for g in range(n_groups):
        for r in range(4):
            lt = stat_ref[g * 4 + r].T
            for a in range(4):
                lse_ref[4 * a + r, rows(g), :] = lt[a * ni:(a + 1) * ni, :]


def _attn_d16_body(q_ref, k_ref, v_ref, bias_ref, o_ref, lse_ref, stat_ref, st_ref):
    n_sub = q_ref.shape[0]
    units = []
    for p in range(N_PAIRS):
        sl = slice(p * LANES, (p + 1) * LANES)
        for j in range(n_sub):
            def write(o, j=j, sl=sl):
                o_ref[j, :, sl] = o

            units.append(dict(
                q=lambda j=j, sl=sl: q_ref[j, :, sl], k=lambda j=j, sl=sl: k_ref[j, :, sl],
                vt=lambda j=j, sl=sl: v_ref[j, :, sl].T,
                bias=lambda: bias_ref[0], slot=j, pair=p, write=write))
    _attn_units(units, stat_ref, st_ref)
    for j in range(n_sub):
        lse_ref[j] = stat_ref[j].T


def _attn_scratch(n_sub, nk, lookahead=SCORE_LOOKAHEAD):
    return [pltpu.VMEM((n_sub, LANES, BAND_BLOCK), F32),
            pltpu.VMEM((lookahead + 1, nk, 2 * BAND_BLOCK), F32)]


def _band_bias(kk, qi, need_prev):
    ok = (kk[:, None] >= qi[None, :]) & (kk[:, None] <= qi[None, :] + BAND_BLOCK)
    if need_prev is not None:
        ok = ok & need_prev[:, None]
    return np.where(ok, 0.0, -np.inf).astype(np.float32)


def _attn_d1(q, k, v, batch, seq):
    blk = BAND_BLOCK
    w = ATT_WIDTH
    nt = seq // ATT_ROWS
    per = ATT_ROWS // blk
    kk = np.arange(2 * blk)
    qi = np.arange(blk)
    bias = jnp.asarray(np.stack([_band_bias(kk, qi, kk >= blk), _band_bias(kk, qi, None)]))
    cur = lambda n: pl.BlockSpec((ATT_ROWS, n), lambda b, i: (b * nt + i, 0))
    prev = pl.BlockSpec((blk, w), lambda b, i: (b * nt * per + jnp.maximum(i * per - 1, 0), 0))
    return pl.pallas_call(
        _attn_d1_body, grid=(batch, nt),
        in_specs=[cur(w), cur(w), prev, cur(w), prev, _const_spec(bias.shape)],
        out_specs=[cur(w), cur(LANES)],
        out_shape=[jax.ShapeDtypeStruct((batch * seq, w), BF16),
                   jax.ShapeDtypeStruct((batch * seq, LANES), F32)],
        scratch_shapes=_attn_scratch(per, 2 * blk, SCORE_LOOKAHEAD_D1),
        compiler_params=_params(2), name="attn_d1")(q, k, k, v, v, bias)


def _attn_d4(q16, k16, v16, batch, seq):
    blk = BAND_BLOCK
    w = ATT_WIDTH
    ni = blk // 4
    nb = seq // 4 // blk
    pos = np.arange(blk)
    true_i = 4 * (pos % ni) + pos // ni
    kk = np.concatenate([true_i, true_i + blk])
    bias = jnp.asarray(np.stack([_band_bias(kk, true_i, kk >= blk), _band_bias(kk, true_i, None)]))
    per = ATT_ROWS // (4 * blk)
    cur = lambda n: pl.BlockSpec((None, RES, per * ni, n), lambda b, i: (b, 0, i, 0))
    prev = pl.BlockSpec((None, RES, ni, w), lambda b, i: (b, 0, jnp.maximum(i * per - 1, 0), 0))
    return pl.pallas_call(
        _attn_d4_body, grid=(batch, nb // per),
        in_specs=[cur(w), cur(w), prev, cur(w), prev, _const_spec(bias.shape)],
        out_specs=[cur(w), cur(LANES)],
        out_shape=[jax.ShapeDtypeStruct((batch, RES, seq // RES, w), BF16),
                   jax.ShapeDtypeStruct((batch, RES, seq // RES, LANES), F32)],
        scratch_shapes=_attn_scratch(4 * per, 2 * blk),
        compiler_params=_params(2), name="attn_d4")(q16, k16, k16, v16, v16, bias)


def _attn_d16(q16, k16, v16, batch, seq):
    blk = BAND_BLOCK
    w = ATT_WIDTH
    n_sub = ATT_ROWS // blk
    assert seq // RES == blk and RES % n_sub == 0
    kk = np.arange(blk)
    bias = jnp.asarray(_band_bias(kk + blk, kk, None)[None])
    spec = lambda n: pl.BlockSpec((None, n_sub, blk, n), lambda b, g: (b, g, 0, 0))
    return pl.pallas_call(
        _attn_d16_body, grid=(batch, RES // n_sub),
        in_specs=[spec(w), spec(w), spec(w), _const_spec(bias.shape)],
        out_specs=[spec(w), spec(LANES)],
        out_shape=[jax.ShapeDtypeStruct((batch, RES, blk, w), BF16),
                   jax.ShapeDtypeStruct((batch, RES, blk, LANES), F32)],
        scratch_shapes=_attn_scratch(n_sub, blk),
        compiler_params=_params(2), name="attn_d16")(q16, k16, v16, bias)


def _outproj_body(x_ref, ys_ref, o1_ref, l1_ref, o4_ref, l4_ref, o16_ref, l16_ref, ehead_ref,
                  w_ref, out_ref, small_ref, perm_ref, perm2_ref):
    tm = x_ref.shape[0]
    ni = tm // RES

    def to_natural(l_ref):
        for r in range(RES):
            small_ref[pl.ds(r, ni, stride=RES), :] = l_ref[r]
        return small_ref[...]

    def to_l16(val):
        small_ref[...] = val
        return jnp.concatenate([small_ref[pl.ds(r, ni, stride=RES), :] for r in range(RES)], axis=0)

    l1 = l1_ref[...]
    l4 = to_natural(l4_ref)
    l16 = to_natural(l16_ref)
    m = jnp.maximum(jnp.maximum(l1, l4), l16)
    e1, e4, e16 = jnp.exp(l1 - m), jnp.exp(l4 - m), jnp.exp(l16 - m)
    inv = 1.0 / (e1 + e4 + e16)
    a1 = _split3_lanes(e1 * inv)
    a4 = _split3_lanes(to_l16(e4 * inv))
    a16 = _split3_lanes(to_l16(e16 * inv))
    acc = x_ref[...] + _dot(ys_ref[...], w_ref[0:SSD_WIDTH, :])
    per = MXU_COLS // LANES
    for c in range(ATT_WIDTH // MXU_COLS):
        cols = slice(c * MXU_COLS, (c + 1) * MXU_COLS)
        ehead = ehead_ref[:, cols]
        o4 = o4_ref[:, :, cols].reshape(tm, MXU_COLS).astype(F32)
        o16 = o16_ref[:, :, cols].reshape(tm, MXU_COLS).astype(F32)
        y_l16 = _dot(a4, ehead) * o4 + _dot(a16, ehead) * o16
        nj = tm // 4
        for s in range(per):
            slab = c * per + s
            for r in range(4):
                for a in range(4):
                    perm2_ref[slab, pl.ds(r * nj + a, ni, stride=4), :] = (
                        y_l16[(4 * a + r) * ni:(4 * a + r + 1) * ni, s * LANES:(s + 1) * LANES])
            for r in range(4):
                perm_ref[slab, pl.ds(r, nj, stride=4), :] = perm2_ref[slab, r * nj:(r + 1) * nj, :]
        y_nat = jnp.concatenate([perm_ref[c * per + s] for s in range(per)], axis=1)
        y_att = _dot(a1, ehead) * o1_ref[:, cols].astype(F32) + y_nat
        acc = acc + _dot(y_att.astype(BF16),
                         w_ref[SSD_WIDTH + c * MXU_COLS:SSD_WIDTH + (c + 1) * MXU_COLS, :])
    out_ref[...] = acc


def _outproj(x, y_ssd, o1, l1, o4, l4, o16, l16, ehead, w, seq):
    t, d = x.shape
    tps = seq // TM
    ni = TM // RES
    row = lambda n: pl.BlockSpec((TM, n), lambda i: (i, 0))
    slab = lambda n: pl.BlockSpec((None, RES, ni, n), lambda i: (i // tps, 0, i % tps, 0))
    return pl.pallas_call(
        _outproj_body, grid=(t // TM,),
        in_specs=[row(d), row(SSD_WIDTH), row(ATT_WIDTH), row(LANES), slab(ATT_WIDTH), slab(LANES),
                  slab(ATT_WIDTH), slab(LANES), _const_spec((LANES, ATT_WIDTH)),
                  _const_spec((SSD_WIDTH + ATT_WIDTH, d))],
        out_specs=row(d), out_shape=jax.ShapeDtypeStruct((t, d), F32),
        scratch_shapes=[pltpu.VMEM((TM, LANES), F32),
                        pltpu.VMEM((ATT_WIDTH // LANES, TM, LANES), F32),
                        pltpu.VMEM((ATT_WIDTH // LANES, TM, LANES), F32)],
        compiler_params=_params(1), name="outproj")(
            x, y_ssd, o1, l1, o4, l4, o16, l16, ehead, w)


def _head_expand(width_per_head):
    r = jnp.arange(LANES)[:, None]
    c = jnp.arange(SSD_HEADS * width_per_head)[None, :]
    return ((r < HEAD_REP * SSD_HEADS) & (r % SSD_HEADS == c // width_per_head)).astype(BF16)


def _rep_heads(v):
    return jnp.pad(jnp.tile(v.astype(F32), HEAD_REP), (0, LANES - HEAD_REP * SSD_HEADS))[None, :]


def kernel(x, ffn1_norm, ffn1_w_gate, ffn1_w_up, ffn1_w_down, mix_norm, w_in, conv_w, conv_b,
           dt_bias, a_log, d_skip, ssd_norm, q_norm, k_norm, w_out, ffn2_norm, ffn2_w_gate,
           ffn2_w_up, ffn2_w_down):
    batch, seq, d = x.shape
    depth = w_in.shape[0]
    assert d == D_MODEL and seq == RES * BAND_BLOCK
    assert all(wd // dl == BAND_BLOCK for wd, dl in ATT_BRANCHES)
    assert seq % TM == 0 and seq % ATT_ROWS == 0 and seq % SSD_ROWS == 0
    t = batch * seq

    tril = jnp.tril(jnp.ones((SSD_CHUNK, SSD_CHUNK), BF16))
    eye = jnp.eye(SSD_CHUNK, dtype=BF16)
    ehead = _head_expand(SSD_HEAD_DIM)
    masked = jnp.where(tril > 0, 0.0, MASKED).astype(BF16)
    ecol = jnp.concatenate([_head_expand(SSD_CHUNK), jnp.tile(masked, (1, SSD_HEADS))], axis=0)
    half = jnp.arange(LANES) // ATT_HEAD_DIM
    bd = (half[:, None] == half[None, :]).astype(F32) / ATT_HEAD_DIM
    bd2 = jnp.concatenate([bd, bd], axis=0).astype(BF16)
    scale = LOG2E / math.sqrt(ATT_HEAD_DIM)

    xf = x.reshape(t, d)
    for i in range(depth):
        dt_cols = jnp.pad(jnp.tile(w_in[i][:, COL_DT:COL_DT + SSD_HEADS], (1, HEAD_REP)),
                          ((0, 0), (0, LANES - HEAD_REP * SSD_HEADS)))
        w_ssd = jnp.concatenate([w_in[i][:, :COL_DT], dt_cols], axis=1).astype(BF16)
        w_att = w_in[i][:, COL_DT + SSD_HEADS:].astype(BF16)
        qg = jnp.tile(q_norm[i].astype(F32) * scale, LANES // ATT_HEAD_DIM)[None, :]
        kg = jnp.tile(k_norm[i].astype(F32), LANES // ATT_HEAD_DIM)[None, :]

        xf = _ffn(xf, ffn1_norm[i][None, :], ffn1_w_gate[i].astype(BF16),
                  ffn1_w_up[i].astype(BF16), ffn1_w_down[i].astype(BF16))
        z, xs, bc, dt = _proj_ssd(xf, mix_norm[i][None, :], w_ssd, conv_w[i], conv_b[i][None, :], seq)
        qn, kn, vn, q16, k16, v16 = _proj_att(xf, mix_norm[i][None, :], w_att, qg, kg, bd2,
                                              batch, seq)
        y_ssd = _ssd(xs, bc, dt, z, _rep_heads(dt_bias[i]), _rep_heads(a_log[i]),
                     jnp.repeat(d_skip[i].astype(F32), SSD_HEAD_DIM)[None, :],
                     ssd_norm[i][None, :], tril, eye, ehead, ecol, batch, seq)
        o1, l1 = _attn_d1(qn, kn, vn, batch, seq)
        o4, l4 = _attn_d4(q16, k16, v16, batch, seq)
        o16, l16 = _attn_d16(q16, k16, v16, batch, seq)
        xf = _outproj(xf, y_ssd, o1, l1, o4, l4, o16, l16, ehead, w_out[i].astype(BF16), seq)
        xf = _ffn(xf, ffn2_norm[i][None, :], ffn2_w_gate[i].astype(BF16),
                  ffn2_w_up[i].astype(BF16), ffn2_w_down[i].astype(BF16))
    return xf.reshape(batch, seq, d)
```

```python
import functools
import math

import numpy as np
import jax
import jax.numpy as jnp
from jax import lax
from jax.experimental import pallas as pl
from jax.experimental.pallas import tpu as pltpu

F32 = jnp.float32
BF16 = jnp.bfloat16

D_MODEL = 1024
SSD_HEADS = 16
SSD_HEAD_DIM = 64
SSD_WIDTH = SSD_HEADS * SSD_HEAD_DIM
SSD_GROUPS = 4
SSD_STATE = 128
CONV_K = 4
SSD_CHUNK = 128
BC_WIDTH = 2 * SSD_GROUPS * SSD_STATE
CONV_DIM = SSD_WIDTH + BC_WIDTH
ATT_HEADS = 16
ATT_HEAD_DIM = 64
ATT_WIDTH = ATT_HEADS * ATT_HEAD_DIM
ATT_BRANCHES = ((128, 1), (512, 4), (2048, 16))
BAND_BLOCK = 128
EPS = 1e-6

LANES = 128
SUBLANES = 8
MXU_COLS = 256
PROJ_COLS = 256
VMEM_LIMIT_BYTES = 56 * 1024 * 1024

LOG2E = math.log2(math.e)
LN2 = math.log(2.0)
MASKED = -1e30
DEN_ROWS = 16
SCORE_LOOKAHEAD = 6
SCORE_LOOKAHEAD_D1 = 10
DENSE_LOOKAHEAD = 2

HEAD_REP = 3
RES = 16
N_PAIRS = ATT_HEADS // 2

COL_Z = 0
COL_XBC = COL_Z + SSD_WIDTH
COL_DT = COL_XBC + CONV_DIM
SSD_COLS = COL_DT + LANES

TM = 1024
ATT_ROWS = 2048
SSD_ROWS = 1024

def _const_spec(shape):
    nd = len(shape)
    return pl.BlockSpec(shape, lambda *_: (0,) * nd, pipeline_mode=pl.Buffered(1))


def _params(n_axes):
    return pltpu.CompilerParams(dimension_semantics=("arbitrary",) * n_axes,
                                vmem_limit_bytes=VMEM_LIMIT_BYTES)


def _rms(x, gain):
    ms = jnp.mean(x * x, axis=-1, keepdims=True)
    return x * lax.rsqrt(ms + EPS) * gain


def _silu(x):
    h = 0.5 * x
    return h + h * jnp.tanh(h)


def _dot(a, b):
    return jnp.dot(a, b, preferred_element_type=F32)


def _dot_nt(a, b):
    return lax.dot_general(a, b, (((1,), (1,)), ((), ())), preferred_element_type=F32)


def _dot_tn(a, b):
    return lax.dot_general(a, b, (((0,), (0,)), ((), ())), preferred_element_type=F32)


def _split3(v):
    hi = v.astype(BF16)
    r1 = v - hi.astype(F32)
    mid = r1.astype(BF16)
    lo = (r1 - mid.astype(F32)).astype(BF16)
    return hi, mid, lo


def _split3_lanes(v):
    hi, mid, lo = _split3(v)
    lane = lax.broadcasted_iota(jnp.int32, v.shape, 1)
    zero = jnp.zeros_like(hi)
    return jnp.where(lane < SSD_HEADS, hi,
                     jnp.where(lane < 2 * SSD_HEADS, mid,
                               jnp.where(lane < 3 * SSD_HEADS, lo, zero)))


def _ffn_body(x_ref, g_ref, wg_ref, wu_ref, wd_ref, o_ref):
    x = x_ref[...]
    xn = _rms(x, g_ref[...]).astype(BF16)
    gate = _dot(xn, wg_ref[...])
    up = _dot(xn, wu_ref[...])
    h = (_silu(gate) * up).astype(BF16)
    o_ref[...] = x + 0.5 * _dot(h, wd_ref[...])


def _ffn(x, gain, wg, wu, wd):
    t, d = x.shape
    f = wg.shape[1]
    row = pl.BlockSpec((TM, d), lambda i: (i, 0))
    return pl.pallas_call(
        _ffn_body, grid=(t // TM,),
        in_specs=[row, _const_spec((1, d)), _const_spec((d, f)), _const_spec((d, f)),
                  _const_spec((f, d))],
        out_specs=row, out_shape=jax.ShapeDtypeStruct((t, d), F32),
        compiler_params=_params(1), name="ffn")(x, gain, wg, wu, wd)


def _proj_ssd_body(x_ref, g_ref, w_ref, cw_ref, cb_ref, z_ref, xs_ref, bc_ref, dt_ref, cbuf_ref,
                   xn_ref, *, tiles_per_seq):
    tm = x_ref.shape[0]
    n_chunks = CONV_DIM // PROJ_COLS
    per = PROJ_COLS // LANES
    xn_ref[...] = _rms(x_ref[...], g_ref[...]).astype(BF16)

    @pl.when(pl.program_id(0) % tiles_per_seq == 0)
    def _():
        cbuf_ref[:, 0:SUBLANES, :] = jnp.zeros((CONV_DIM // LANES, SUBLANES, LANES), F32)

    main = lambda c: _dot(
        xn_ref[...], w_ref[:, COL_XBC + c * PROJ_COLS:COL_XBC + (c + 1) * PROJ_COLS])
    z_chunks = SSD_WIDTH // PROJ_COLS
    z_every = n_chunks // z_chunks
    dt_ref[...] = _dot(xn_ref[...], w_ref[:, COL_DT:SSD_COLS])
    pending = [main(c) for c in range(DENSE_LOOKAHEAD)]
    for c in range(n_chunks):
        res = pending.pop(0)
        if c + DENSE_LOOKAHEAD < n_chunks:
            pending.append(main(c + DENSE_LOOKAHEAD))
        if c % z_every == 0:
            zc = slice(COL_Z + (c // z_every) * PROJ_COLS, COL_Z + (c // z_every + 1) * PROJ_COLS)
            z_ref[:, zc] = _dot(xn_ref[...], w_ref[:, zc]).astype(BF16)
        for s in range(per):
            slab = c * per + s
            lanes = slice(slab * LANES, (slab + 1) * LANES)
            cbuf_ref[slab, SUBLANES:SUBLANES + tm, :] = res[:, s * LANES:(s + 1) * LANES]
            conv = cb_ref[:, lanes]
            for k in range(CONV_K):
                off = SUBLANES - (CONV_K - 1) + k
                conv = conv + cw_ref[k:k + 1, lanes] * cbuf_ref[slab, off:off + tm, :]
            cbuf_ref[slab, 0:SUBLANES, :] = cbuf_ref[slab, tm:tm + SUBLANES, :]
            act = conv + conv * jnp.tanh(conv)
            if slab * LANES < SSD_WIDTH:
                xs_ref[:, lanes] = act
            else:
                bc_ref[:, slab * LANES - SSD_WIDTH:(slab + 1) * LANES - SSD_WIDTH] = act.astype(BF16)


def _proj_ssd(x, gain, w, cw, cb, seq):
    t, d = x.shape
    row = lambda n: pl.BlockSpec((TM, n), lambda i: (i, 0))
    shp = lambda n, dt: jax.ShapeDtypeStruct((t, n), dt)
    return pl.pallas_call(
        functools.partial(_proj_ssd_body, tiles_per_seq=seq // TM), grid=(t // TM,),
        in_specs=[row(d), _const_spec((1, d)), _const_spec((d, SSD_COLS)),
                  _const_spec((CONV_K, CONV_DIM)), _const_spec((1, CONV_DIM))],
        out_specs=[row(SSD_WIDTH), row(SSD_WIDTH), row(BC_WIDTH), row(LANES)],
        out_shape=[shp(SSD_WIDTH, BF16), shp(SSD_WIDTH, F32), shp(BC_WIDTH, BF16),
                   shp(LANES, F32)],
        scratch_shapes=[pltpu.VMEM((CONV_DIM // LANES, SUBLANES + TM, LANES), F32),
                        pltpu.VMEM((TM, d), BF16)],
        compiler_params=_params(1), name="proj_ssd")(x, gain, w, cw, cb)


def _proj_att_body(x_ref, g_ref, w_ref, qg_ref, kg_ref, bd2_ref,
                   qn_ref, kn_ref, vn_ref, q16_ref, k16_ref, v16_ref, scr_ref, scr2_ref, xn_ref):
    tm = x_ref.shape[0]
    ni = tm // RES
    per = PROJ_COLS // LANES
    n_slabs = scr_ref.shape[0]
    xn_ref[...] = _rms(x_ref[...], g_ref[...]).astype(BF16)
    bd2 = bd2_ref[...]
    plan = ((qg_ref, qn_ref, q16_ref), (kg_ref, kn_ref, k16_ref), (None, vn_ref, v16_ref))
    chunks = [(t, c) for t in range(3) for c in range(ATT_WIDTH // PROJ_COLS)]
    main = lambda t, c: _dot(
        xn_ref[...], w_ref[:, t * ATT_WIDTH + c * PROJ_COLS:t * ATT_WIDTH + (c + 1) * PROJ_COLS])
    pending = [main(*ch) for ch in chunks[:DENSE_LOOKAHEAD]]
    for ci, (t, c) in enumerate(chunks):
        gain_ref, nat_ref, l16_ref = plan[t]
        res = pending.pop(0)
        if ci + DENSE_LOOKAHEAD < len(chunks):
            pending.append(main(*chunks[ci + DENSE_LOOKAHEAD]))
        for s in range(per):
            lanes = slice(c * PROJ_COLS + s * LANES, c * PROJ_COLS + (s + 1) * LANES)
            y = res[:, s * LANES:(s + 1) * LANES]
            if gain_ref is not None:
                sq = y * y
                hi = sq.astype(BF16)
                lo = (sq - hi.astype(F32)).astype(BF16)
                ms = _dot(jnp.concatenate([hi, lo], axis=1), bd2)
                y = y * lax.rsqrt(ms + EPS) * gain_ref[...]
            nat_ref[:, lanes] = y.astype(BF16)
            slab = (ci * per + s) % n_slabs
            scr_ref[slab] = y
            nj = tm // 4
            for r in range(4):
                scr2_ref[slab, r * nj:(r + 1) * nj, :] = scr_ref[slab, pl.ds(r, nj, stride=4), :]
            for r in range(4):
                for a in range(4):
                    l16_ref[4 * a + r, :, lanes] = (
                        scr2_ref[slab, pl.ds(r * nj + a, ni, stride=4), :].astype(BF16))


def _proj_att(x, gain, w, qg, kg, bd2, batch, seq):
    t, d = x.shape
    tps = seq // TM
    ni = TM // RES
    row = lambda n: pl.BlockSpec((TM, n), lambda i: (i, 0))
    l16 = pl.BlockSpec((None, RES, ni, ATT_WIDTH), lambda i: (i // tps, 0, i % tps, 0))
    nat_shape = jax.ShapeDtypeStruct((t, ATT_WIDTH), BF16)
    l16_shape = jax.ShapeDtypeStruct((batch, RES, seq // RES, ATT_WIDTH), BF16)
    return pl.pallas_call(
        _proj_att_body, grid=(t // TM,),
        in_specs=[row(d), _const_spec((1, d)), _const_spec((d, 3 * ATT_WIDTH)),
                  _const_spec((1, LANES)), _const_spec((1, LANES)),
                  _const_spec((2 * LANES, LANES))],
        out_specs=[row(ATT_WIDTH)] * 3 + [l16] * 3,
        out_shape=[nat_shape] * 3 + [l16_shape] * 3,
        scratch_shapes=[pltpu.VMEM((2 * PROJ_COLS // LANES, TM, LANES), F32),
                        pltpu.VMEM((2 * PROJ_COLS // LANES, TM, LANES), F32),
                        pltpu.VMEM((TM, d), BF16)],
        compiler_params=_params(1), name="proj_att")(x, gain, w, qg, kg, bd2)


def _ssd_body(xs_ref, bc_ref, dt_ref, z_ref, dtb_ref, alog_ref, dskip_ref, nw_ref,
              tril_ref, eye_ref, ehead_ref, ecol_ref, y_ref, state_ref, acumt_ref, *, chunks):
    L = SSD_CHUNK
    gsz = SSD_GROUPS * SSD_STATE

    @pl.when(pl.program_id(1) == 0)
    def _():
        state_ref[...] = jnp.zeros_like(state_ref)

    lane = lax.broadcasted_iota(jnp.int32, (L, SSD_WIDTH), 1)
    even_head = (lane // SSD_HEAD_DIM) % 2 == 0
    heads_per_group = SSD_HEADS // SSD_GROUPS
    gw = heads_per_group * SSD_HEAD_DIM
    groups = range(SSD_GROUPS)

    def front(ci):
        rows = slice(ci * L, (ci + 1) * L)
        cgs = [bc_ref[rows, gsz + g * SSD_STATE:gsz + (g + 1) * SSD_STATE] for g in groups]
        bgs = [bc_ref[rows, g * SSD_STATE:(g + 1) * SSD_STATE] for g in groups]
        cbms = [_dot_nt(cgs[g], bgs[g]) for g in groups]
        dt = jax.nn.softplus(dt_ref[rows, :] + dtb_ref[...])
        adt = dt * (-jnp.exp(alog_ref[...]) * LOG2E)
        tril = tril_ref[...]
        hi, mid, lo = _split3(adt)
        acum = _dot(tril, hi) + _dot(tril, mid) + _dot(tril, lo)
        acumt_ref[ci] = acum.T
        dt_x = _dot(_split3_lanes(dt), ehead_ref[...])
        acum3 = _split3_lanes(acum)
        acum_x = _dot(acum3, ehead_ref[...])
        acum_col = _dot(jnp.concatenate([acum3, eye_ref[...]], axis=1), ecol_ref[...])
        return cgs, bgs, cbms, dt_x, acum_x, acum_col

    def back(ci, cgs, bgs, cbms, dt_x, acum_x, acum_col):
        rows = slice(ci * L, (ci + 1) * L)
        xs = xs_ref[rows, :]
        sts = [state_ref[:, g * gw:(g + 1) * gw] for g in groups]
        y_offs = [_dot(cgs[g], sts[g].astype(BF16)) for g in groups]
        xdt = xs * dt_x
        eacum_x = jnp.exp2(acum_x)
        tot_x = acum_x[L - 1:L, :]
        xdt_b = xdt.astype(BF16)
        zero = jnp.zeros_like(xdt_b)
        xdt_even = jnp.where(even_head, xdt_b, zero)
        xdt_odd = jnp.where(even_head, zero, xdt_b)
        xdec_b = (xdt * jnp.exp2(tot_x - acum_x)).astype(BF16)
        chunk_decay = eacum_x[L - 1:L, :]
        for g in groups:
            gs = slice(g * gw, (g + 1) * gw)
            new_states = _dot_tn(bgs[g], xdec_b[:, gs])
            state_ref[:, gs] = sts[g] * chunk_decay[:, gs] + new_states
        y_parts = []
        for g in groups:
            gs = slice(g * gw, (g + 1) * gw)
            ms = []
            for r in range(heads_per_group):
                h = g * heads_per_group + r
                seg = acum_col[:, h * L:(h + 1) * L] - acumt_ref[ci, h:h + 1, :]
                ms.append((cbms[g] * jnp.exp2(seg)).astype(BF16))
            yd = []
            for pr in range(heads_per_group // 2):
                pair = g * (heads_per_group // 2) + pr
                sl = slice(pair * LANES, (pair + 1) * LANES)
                lhs = jnp.concatenate([ms[2 * pr], ms[2 * pr + 1]], axis=1)
                rhs = jnp.concatenate([xdt_even[:, sl], xdt_odd[:, sl]], axis=0)
                yd.append(_dot(lhs, rhs))
            y_parts.append(jnp.concatenate(yd, axis=1) + y_offs[g] * eacum_x[:, gs])

        y = jnp.concatenate(y_parts, axis=1) + xs * dskip_ref[...]
        y = y * _silu(z_ref[rows, :]).astype(F32)
        outs = []
        for g in groups:
            yg = y[:, g * gw:(g + 1) * gw]
            msq = jnp.mean(yg * yg, axis=-1, keepdims=True)
            outs.append(yg * lax.rsqrt(msq + EPS))
        y_ref[rows, :] = (jnp.concatenate(outs, axis=1) * nw_ref[...]).astype(BF16)

    pending = front(0)
    for ci in range(chunks):
        ready = pending
        if ci + 1 < chunks:
            pending = front(ci + 1)
        back(ci, *ready)


def _ssd(xs, bc, dt, z, dtb, alog, dskip_x, nw, tril, eye, ehead, ecol, batch, seq):
    L = SSD_CHUNK
    ns = seq // SSD_ROWS
    chunks = SSD_ROWS // L
    blk = lambda n: pl.BlockSpec((SSD_ROWS, n), lambda b, s: (b * ns + s, 0))
    return pl.pallas_call(
        functools.partial(_ssd_body, chunks=chunks), grid=(batch, ns),
        in_specs=[blk(SSD_WIDTH), blk(BC_WIDTH), blk(LANES), blk(SSD_WIDTH),
                  _const_spec((1, LANES)), _const_spec((1, LANES)),
                  _const_spec((1, SSD_WIDTH)), _const_spec((1, SSD_WIDTH)),
                  _const_spec((L, L)), _const_spec((L, L)), _const_spec((LANES, SSD_WIDTH)),
                  _const_spec((LANES + L, SSD_HEADS * L))],
        out_specs=blk(SSD_WIDTH),
        out_shape=jax.ShapeDtypeStruct((batch * seq, SSD_WIDTH), BF16),
        scratch_shapes=[pltpu.VMEM((SSD_STATE, SSD_WIDTH), F32),
                        pltpu.VMEM((chunks, L, LANES), F32)],
        compiler_params=_params(2), name="ssd")(
            xs, bc, dt, z, dtb, alog, dskip_x, nw, tril, eye, ehead, ecol)


def _attn_units(units, stat_ref, st_ref):
    blk = BAND_BLOCK
    lane = lax.broadcasted_iota(jnp.int32, (blk, LANES), 1)
    first_head = lane < ATT_HEAD_DIM
    srow = lax.broadcasted_iota(jnp.int32, (LANES, blk), 0)
    first_rows = srow < ATT_HEAD_DIM
    stat_ref[...] = jnp.zeros_like(stat_ref)

    def scores(unit):
        q2 = unit["q"]()
        zero = jnp.zeros_like(q2)
        qst = jnp.concatenate([jnp.where(first_head, q2, zero),
                               jnp.where(first_head, zero, q2)], axis=0)
        return _dot_nt(unit["k"](), qst)

    n_slots = st_ref.shape[0]
    lookahead = n_slots - 1

    def issue(idx):
        st_ref[idx % n_slots] = scores(units[idx])

    for idx in range(min(lookahead, len(units))):
        issue(idx)
    for idx, unit in enumerate(units):
        if idx + lookahead < len(units):
            issue(idx + lookahead)
        bias = unit["bias"]()
        ps, mxs = [], []
        for hh in range(2):
            s = st_ref[idx % n_slots, :, hh * blk:(hh + 1) * blk] + bias
            mx = jnp.max(s, axis=0, keepdims=True)
            ps.append(jnp.exp2(s - mx).astype(BF16))
            mxs.append(mx)
        vt = unit["vt"]()
        ones = jnp.ones((DEN_ROWS, vt.shape[1]), BF16)
        ot = _dot(jnp.concatenate([vt, ones], axis=0), jnp.concatenate(ps, axis=1))
        outs = []
        for hh in range(2):
            den = ot[LANES:LANES + 1, hh * blk:(hh + 1) * blk]
            outs.append(ot[0:LANES, hh * blk:(hh + 1) * blk] * (1.0 / den))
            lse = (mxs[hh] + jnp.log2(den)) * LN2
            for rep in range(HEAD_REP):
                row = rep * ATT_HEADS + 2 * unit["pair"] + hh
                stat_ref[unit["slot"], row:row + 1, :] = lse
        o_t = jnp.where(first_rows, outs[0], outs[1])
        unit["write"](o_t.T.astype(BF16))


def _attn_d1_body(q_ref, kc_ref, kp_ref, vc_ref, vp_ref, bias_ref, o_ref, lse_ref, stat_ref,
                  st_ref):
    blk = BAND_BLOCK
    n_sub = ATT_ROWS // blk
    has_prev = jnp.where(pl.program_id(1) > 0, 1, 0)
    units = []
    for p in range(N_PAIRS):
        sl = slice(p * LANES, (p + 1) * LANES)
        vts = {}

        def vt_block(m, sl=sl, vts=vts):
            if m not in vts:
                v = vp_ref[:, sl] if m == 0 else vc_ref[(m - 1) * blk:m * blk, sl]
                vts[m] = v.T
            return vts[m]

        for j in range(n_sub):
            def write(o, j=j, sl=sl):
                o_ref[j * blk:(j + 1) * blk, sl] = o

            if j == 0:
                k_of = lambda sl=sl: jnp.concatenate([kp_ref[:, sl], kc_ref[0:blk, sl]], axis=0)
                bias_of = lambda: bias_ref[has_prev]
            else:
                k_of = lambda j=j, sl=sl: kc_ref[(j - 1) * blk:(j + 1) * blk, sl]
                bias_of = lambda: bias_ref[1]
            units.append(dict(
                q=lambda j=j, sl=sl: q_ref[j * blk:(j + 1) * blk, sl], k=k_of,
                vt=lambda j=j, f=vt_block: jnp.concatenate([f(j), f(j + 1)], axis=1),
                bias=bias_of, slot=j, pair=p, write=write))
    _attn_units(units, stat_ref, st_ref)
    for j in range(n_sub):
        lse_ref[j * blk:(j + 1) * blk, :] = stat_ref[j].T


def _attn_d4_body(q_ref, kc_ref, kp_ref, vc_ref, vp_ref, bias_ref, o_ref, lse_ref, stat_ref,
                  st_ref):
    blk = BAND_BLOCK
    ni = blk // 4
    n_groups = q_ref.shape[1] // ni
    has_prev = jnp.where(pl.program_id(1) > 0, 1, 0)
    rows = lambda g: slice(g * ni, (g + 1) * ni)
    gather = lambda ref, r, g, sl: [ref[4 * a + r, rows(g), sl] for a in range(4)]

    def window(cur_ref, prev_ref, r, g, sl):
        before = gather(prev_ref, r, 0, sl) if g == 0 else gather(cur_ref, r, g - 1, sl)
        return jnp.concatenate(before, axis=0), jnp.concatenate(gather(cur_ref, r, g, sl), axis=0)

    units = []
    for p in range(N_PAIRS):
        sl = slice(p * LANES, (p + 1) * LANES)
        for g in range(n_groups):
            for r in range(4):
                def write(o, r=r, g=g, sl=sl):
                    for a in range(4):
                        o_ref[4 * a + r, rows(g), sl] = o[a * ni:(a + 1) * ni, :]

                units.append(dict(
                    q=lambda r=r, g=g, sl=sl: jnp.concatenate(gather(q_ref, r, g, sl), axis=0),
                    k=lambda r=r, g=g, sl=sl: jnp.concatenate(
                        window(kc_ref, kp_ref, r, g, sl), axis=0),
                    vt=lambda r=r, g=g, sl=sl: jnp.concatenate(
                        [v.T for v in window(vc_ref, vp_ref, r, g, sl)], axis=1),
                    bias=(lambda: bias_ref[has_prev]) if g == 0 else (lambda: bias_ref[1]),
                    slot=g * 4 + r, pair=p, write=write))
    _attn_units(units, stat_ref, st_ref)
    for g in range(n_groups):
        for r in range(4):
            lt = stat_ref[g * 4 + r].T
            for a in range(4):
                lse_ref[4 * a + r, rows(g), :] = lt[a * ni:(a + 1) * ni, :]


def _attn_d16_body(q_ref, k_ref, v_ref, bias_ref, o_ref, lse_ref, stat_ref, st_ref):
    n_sub = q_ref.shape[0]
    units = []
    for p in range(N_PAIRS):
        sl = slice(p * LANES, (p + 1) * LANES)
        for j in range(n_sub):
            def write(o, j=j, sl=sl):
                o_ref[j, :, sl] = o

            units.append(dict(
                q=lambda j=j, sl=sl: q_ref[j, :, sl], k=lambda j=j, sl=sl: k_ref[j, :, sl],
                vt=lambda j=j, sl=sl: v_ref[j, :, sl].T,
                bias=lambda: bias_ref[0], slot=j, pair=p, write=write))
    _attn_units(units, stat_ref, st_ref)
    for j in range(n_sub):
        lse_ref[j] = stat_ref[j].T


def _attn_scratch(n_sub, nk, lookahead=SCORE_LOOKAHEAD):
    return [pltpu.VMEM((n_sub, LANES, BAND_BLOCK), F32),
            pltpu.VMEM((lookahead + 1, nk, 2 * BAND_BLOCK), F32)]


def _band_bias(kk, qi, need_prev):
    ok = (kk[:, None] >= qi[None, :]) & (kk[:, None] <= qi[None, :] + BAND_BLOCK)
    if need_prev is not None:
        ok = ok & need_prev[:, None]
    return np.where(ok, 0.0, -np.inf).astype(np.float32)


def _attn_d1(q, k, v, batch, seq):
    blk = BAND_BLOCK
    w = ATT_WIDTH
    nt = seq // ATT_ROWS
    per = ATT_ROWS // blk
    kk = np.arange(2 * blk)
    qi = np.arange(blk)
    bias = jnp.asarray(np.stack([_band_bias(kk, qi, kk >= blk), _band_bias(kk, qi, None)]))
    cur = lambda n: pl.BlockSpec((ATT_ROWS, n), lambda b, i: (b * nt + i, 0))
    prev = pl.BlockSpec((blk, w), lambda b, i: (b * nt * per + jnp.maximum(i * per - 1, 0), 0))
    return pl.pallas_call(
        _attn_d1_body, grid=(batch, nt),
        in_specs=[cur(w), cur(w), prev, cur(w), prev, _const_spec(bias.shape)],
        out_specs=[cur(w), cur(LANES)],
        out_shape=[jax.ShapeDtypeStruct((batch * seq, w), BF16),
                   jax.ShapeDtypeStruct((batch * seq, LANES), F32)],
        scratch_shapes=_attn_scratch(per, 2 * blk, SCORE_LOOKAHEAD_D1),
        compiler_params=_params(2), name="attn_d1")(q, k, k, v, v, bias)


def _attn_d4(q16, k16, v16, batch, seq):
    blk = BAND_BLOCK
    w = ATT_WIDTH
    ni = blk // 4
    nb = seq // 4 // blk
    pos = np.arange(blk)
    true_i = 4 * (pos % ni) + pos // ni
    kk = np.concatenate([true_i, true_i + blk])
    bias = jnp.asarray(np.stack([_band_bias(kk, true_i, kk >= blk), _band_bias(kk, true_i, None)]))
    per = ATT_ROWS // (4 * blk)
    cur = lambda n: pl.BlockSpec((None, RES, per * ni, n), lambda b, i: (b, 0, i, 0))
    prev = pl.BlockSpec((None, RES, ni, w), lambda b, i: (b, 0, jnp.maximum(i * per - 1, 0), 0))
    return pl.pallas_call(
        _attn_d4_body, grid=(batch, nb // per),
        in_specs=[cur(w), cur(w), prev, cur(w), prev, _const_spec(bias.shape)],
        out_specs=[cur(w), cur(LANES)],
        out_shape=[jax.ShapeDtypeStruct((batch, RES, seq // RES, w), BF16),
                   jax.ShapeDtypeStruct((batch, RES, seq // RES, LANES), F32)],
        scratch_shapes=_attn_scratch(4 * per, 2 * blk),
        compiler_params=_params(2), name="attn_d4")(q16, k16, k16, v16, v16, bias)


def _attn_d16(q16, k16, v16, batch, seq):
    blk = BAND_BLOCK
    w = ATT_WIDTH
    n_sub = ATT_ROWS // blk
    assert seq // RES == blk and RES % n_sub == 0
    kk = np.arange(blk)
    bias = jnp.asarray(_band_bias(kk + blk, kk, None)[None])
    spec = lambda n: pl.BlockSpec((None, n_sub, blk, n), lambda b, g: (b, g, 0, 0))
    return pl.pallas_call(
        _attn_d16_body, grid=(batch, RES // n_sub),
        in_specs=[spec(w), spec(w), spec(w), _const_spec(bias.shape)],
        out_specs=[spec(w), spec(LANES)],
        out_shape=[jax.ShapeDtypeStruct((batch, RES, blk, w), BF16),
                   jax.ShapeDtypeStruct((batch, RES, blk, LANES), F32)],
        scratch_shapes=_attn_scratch(n_sub, blk),
        compiler_params=_params(2), name="attn_d16")(q16, k16, v16, bias)


def _outproj_body(x_ref, ys_ref, o1_ref, l1_ref, o4_ref, l4_ref, o16_ref, l16_ref, ehead_ref,
                  w_ref, out_ref, small_ref, perm_ref, perm2_ref):
    tm = x_ref.shape[0]
    ni = tm // RES

    def to_natural(l_ref):
        for r in range(RES):
            small_ref[pl.ds(r, ni, stride=RES), :] = l_ref[r]
        return small_ref[...]

    def to_l16(val):
        small_ref[...] = val
        return jnp.concatenate([small_ref[pl.ds(r, ni, stride=RES), :] for r in range(RES)], axis=0)

    l1 = l1_ref[...]
    l4 = to_natural(l4_ref)
    l16 = to_natural(l16_ref)
    m = jnp.maximum(jnp.maximum(l1, l4), l16)
    e1, e4, e16 = jnp.exp(l1 - m), jnp.exp(l4 - m), jnp.exp(l16 - m)
    inv = 1.0 / (e1 + e4 + e16)
    a1 = _split3_lanes(e1 * inv)
    a4 = _split3_lanes(to_l16(e4 * inv))
    a16 = _split3_lanes(to_l16(e16 * inv))
    acc = x_ref[...] + _dot(ys_ref[...], w_ref[0:SSD_WIDTH, :])
    per = MXU_COLS // LANES
    for c in range(ATT_WIDTH // MXU_COLS):
        cols = slice(c * MXU_COLS, (c + 1) * MXU_COLS)
        ehead = ehead_ref[:, cols]
        o4 = o4_ref[:, :, cols].reshape(tm, MXU_COLS).astype(F32)
        o16 = o16_ref[:, :, cols].reshape(tm, MXU_COLS).astype(F32)
        y_l16 = _dot(a4, ehead) * o4 + _dot(a16, ehead) * o16
        nj = tm // 4
        for s in range(per):
            slab = c * per + s
            for r in range(4):
                for a in range(4):
                    perm2_ref[slab, pl.ds(r * nj + a, ni, stride=4), :] = (
                        y_l16[(4 * a + r) * ni:(4 * a + r + 1) * ni, s * LANES:(s + 1) * LANES])
            for r in range(4):
                perm_ref[slab, pl.ds(r, nj, stride=4), :] = perm2_ref[slab, r * nj:(r + 1) * nj, :]
        y_nat = jnp.concatenate([perm_ref[c * per + s] for s in range(per)], axis=1)
        y_att = _dot(a1, ehead) * o1_ref[:, cols].astype(F32) + y_nat
        acc = acc + _dot(y_att.astype(BF16),
                         w_ref[SSD_WIDTH + c * MXU_COLS:SSD_WIDTH + (c + 1) * MXU_COLS, :])
    out_ref[...] = acc


def _outproj(x, y_ssd, o1, l1, o4, l4, o16, l16, ehead, w, seq):
    t, d = x.shape
    tps = seq // TM
    ni = TM // RES
    row = lambda n: pl.BlockSpec((TM, n), lambda i: (i, 0))
    slab = lambda n: pl.BlockSpec((None, RES, ni, n), lambda i: (i // tps, 0, i % tps, 0))
    return pl.pallas_call(
        _outproj_body, grid=(t // TM,),
        in_specs=[row(d), row(SSD_WIDTH), row(ATT_WIDTH), row(LANES), slab(ATT_WIDTH), slab(LANES),
                  slab(ATT_WIDTH), slab(LANES), _const_spec((LANES, ATT_WIDTH)),
                  _const_spec((SSD_WIDTH + ATT_WIDTH, d))],
        out_specs=row(d), out_shape=jax.ShapeDtypeStruct((t, d), F32),
        scratch_shapes=[pltpu.VMEM((TM, LANES), F32),
                        pltpu.VMEM((ATT_WIDTH // LANES, TM, LANES), F32),
                        pltpu.VMEM((ATT_WIDTH // LANES, TM, LANES), F32)],
        compiler_params=_params(1), name="outproj")(
            x, y_ssd, o1, l1, o4, l4, o16, l16, ehead, w)


def _head_expand(width_per_head):
    r = jnp.arange(LANES)[:, None]
    c = jnp.arange(SSD_HEADS * width_per_head)[None, :]
    return ((r < HEAD_REP * SSD_HEADS) & (r % SSD_HEADS == c // width_per_head)).astype(BF16)


def _rep_heads(v):
    return jnp.pad(jnp.tile(v.astype(F32), HEAD_REP), (0, LANES - HEAD_REP * SSD_HEADS))[None, :]


def kernel(x, ffn1_norm, ffn1_w_gate, ffn1_w_up, ffn1_w_down, mix_norm, w_in, conv_w, conv_b,
           dt_bias, a_log, d_skip, ssd_norm, q_norm, k_norm, w_out, ffn2_norm, ffn2_w_gate,
           ffn2_w_up, ffn2_w_down):
    batch, seq, d = x.shape
    depth = w_in.shape[0]
    assert d == D_MODEL and seq == RES * BAND_BLOCK
    assert all(wd // dl == BAND_BLOCK for wd, dl in ATT_BRANCHES)
    assert seq % TM == 0 and seq % ATT_ROWS == 0 and seq % SSD_ROWS == 0
    t = batch * seq

    tril = jnp.tril(jnp.ones((SSD_CHUNK, SSD_CHUNK), BF16))
    eye = jnp.eye(SSD_CHUNK, dtype=BF16)
    ehead = _head_expand(SSD_HEAD_DIM)
    masked = jnp.where(tril > 0, 0.0, MASKED).astype(BF16)
    ecol = jnp.concatenate([_head_expand(SSD_CHUNK), jnp.tile(masked, (1, SSD_HEADS))], axis=0)
    half = jnp.arange(LANES) // ATT_HEAD_DIM
    bd = (half[:, None] == half[None, :]).astype(F32) / ATT_HEAD_DIM
    bd2 = jnp.concatenate([bd, bd], axis=0).astype(BF16)
    scale = LOG2E / math.sqrt(ATT_HEAD_DIM)

    xf = x.reshape(t, d)
    for i in range(depth):
        dt_cols = jnp.pad(jnp.tile(w_in[i][:, COL_DT:COL_DT + SSD_HEADS], (1, HEAD_REP)),
                          ((0, 0), (0, LANES - HEAD_REP * SSD_HEADS)))
        w_ssd = jnp.concatenate([w_in[i][:, :COL_DT], dt_cols], axis=1).astype(BF16)
        w_att = w_in[i][:, COL_DT + SSD_HEADS:].astype(BF16)
        qg = jnp.tile(q_norm[i].astype(F32) * scale, LANES // ATT_HEAD_DIM)[None, :]
        kg = jnp.tile(k_norm[i].astype(F32), LANES // ATT_HEAD_DIM)[None, :]

        xf = _ffn(xf, ffn1_norm[i][None, :], ffn1_w_gate[i].astype(BF16),
                  ffn1_w_up[i].astype(BF16), ffn1_w_down[i].astype(BF16))
        z, xs, bc, dt = _proj_ssd(xf, mix_norm[i][None, :], w_ssd, 0.5 * conv_w[i],
                                  0.5 * conv_b[i][None, :], seq)
        qn, kn, vn, q16, k16, v16 = _proj_att(xf, mix_norm[i][None, :], w_att, qg, kg, bd2,
                                              batch, seq)
        y_ssd = _ssd(xs, bc, dt, z, _rep_heads(dt_bias[i]), _rep_heads(a_log[i]),
                     jnp.repeat(d_skip[i].astype(F32), SSD_HEAD_DIM)[None, :],
                     ssd_norm[i][None, :], tril, eye, ehead, ecol, batch, seq)
        o1, l1 = _attn_d1(qn, kn, vn, batch, seq)
        o4, l4 = _attn_d4(q16, k16, v16, batch, seq)
        o16, l16 = _attn_d16(q16, k16, v16, batch, seq)
        xf = _outproj(xf, y_ssd, o1, l1, o4, l4, o16, l16, ehead, w_out[i].astype(BF16), seq)
        xf = _ffn(xf, ffn2_norm[i][None, :], ffn2_w_gate[i].astype(BF16),
                  ffn2_w_up[i].astype(BF16), ffn2_w_down[i].astype(BF16))
    return xf.reshape(batch, seq, d)
```

```python
import functools
import math

import numpy as np
import jax
import jax.numpy as jnp
from jax import lax
from jax.experimental import pallas as pl
from jax.experimental.pallas import tpu as pltpu

F32 = jnp.float32
BF16 = jnp.bfloat16

D_MODEL = 1024
SSD_HEADS = 16
SSD_HEAD_DIM = 64
SSD_WIDTH = SSD_HEADS * SSD_HEAD_DIM
SSD_GROUPS = 4
SSD_STATE = 128
CONV_K = 4
SSD_CHUNK = 128
BC_WIDTH = 2 * SSD_GROUPS * SSD_STATE
CONV_DIM = SSD_WIDTH + BC_WIDTH
ATT_HEADS = 16
ATT_HEAD_DIM = 64
ATT_WIDTH = ATT_HEADS * ATT_HEAD_DIM
ATT_BRANCHES = ((128, 1), (512, 4), (2048, 16))
BAND_BLOCK = 128
EPS = 1e-6

LANES = 128
SUBLANES = 8
MXU_COLS = 256
PROJ_COLS = 256
VMEM_LIMIT_BYTES = 56 * 1024 * 1024

LOG2E = math.log2(math.e)
LN2 = math.log(2.0)
MASKED = -1e30
DEN_ROWS = 16
SCORE_LOOKAHEAD = 6
SCORE_LOOKAHEAD_D1 = 10
DENSE_LOOKAHEAD = 2

HEAD_REP = 3
RES = 16
N_PAIRS = ATT_HEADS // 2

COL_Z = 0
COL_XBC = COL_Z + SSD_WIDTH
COL_DT = COL_XBC + CONV_DIM
SSD_COLS = COL_DT + LANES

TM = 1024
ATT_ROWS = 2048
SSD_ROWS = 1024

def _const_spec(shape):
    nd = len(shape)
    return pl.BlockSpec(shape, lambda *_: (0,) * nd, pipeline_mode=pl.Buffered(1))


def _params(n_axes):
    return pltpu.CompilerParams(dimension_semantics=("arbitrary",) * n_axes,
                                vmem_limit_bytes=VMEM_LIMIT_BYTES)


def _rms(x, gain):
    ms = jnp.mean(x * x, axis=-1, keepdims=True)
    return x * lax.rsqrt(ms + EPS) * gain


def _silu(x):
    h = 0.5 * x
    return h + h * jnp.tanh(h)


def _dot(a, b):
    return jnp.dot(a, b, preferred_element_type=F32)


def _dot_nt(a, b):
    return lax.dot_general(a, b, (((1,), (1,)), ((), ())), preferred_element_type=F32)


def _dot_tn(a, b):
    return lax.dot_general(a, b, (((0,), (0,)), ((), ())), preferred_element_type=F32)


def _split3(v):
    hi = v.astype(BF16)
    r1 = v - hi.astype(F32)
    mid = r1.astype(BF16)
    lo = (r1 - mid.astype(F32)).astype(BF16)
    return hi, mid, lo


def _split3_lanes(v):
    hi, mid, lo = _split3(v)
    lane = lax.broadcasted_iota(jnp.int32, v.shape, 1)
    zero = jnp.zeros_like(hi)
    return jnp.where(lane < SSD_HEADS, hi,
                     jnp.where(lane < 2 * SSD_HEADS, mid,
                               jnp.where(lane < 3 * SSD_HEADS, lo, zero)))


def _ffn_body(x_ref, g_ref, wg_ref, wu_ref, wd_ref, o_ref):
    x = x_ref[...]
    xn = _rms(x, g_ref[...]).astype(BF16)
    gate = _dot(xn, wg_ref[...])
    up = _dot(xn, wu_ref[...])
    h = (_silu(gate) * up).astype(BF16)
    o_ref[...] = x + 0.5 * _dot(h, wd_ref[...])


def _ffn(x, gain, wg, wu, wd):
    t, d = x.shape
    f = wg.shape[1]
    row = pl.BlockSpec((TM, d), lambda i: (i, 0))
    return pl.pallas_call(
        _ffn_body, grid=(t // TM,),
        in_specs=[row, _const_spec((1, d)), _const_spec((d, f)), _const_spec((d, f)),
                  _const_spec((f, d))],
        out_specs=row, out_shape=jax.ShapeDtypeStruct((t, d), F32),
        compiler_params=_params(1), name="ffn")(x, gain, wg, wu, wd)


def _proj_ssd_body(x_ref, g_ref, w_ref, cw_ref, cb_ref, z_ref, xs_ref, bc_ref, dt_ref, cbuf_ref,
                   xn_ref, *, tiles_per_seq):
    tm = x_ref.shape[0]
    n_chunks = CONV_DIM // PROJ_COLS
    per = PROJ_COLS // LANES
    xn_ref[...] = _rms(x_ref[...], g_ref[...]).astype(BF16)

    @pl.when(pl.program_id(0) % tiles_per_seq == 0)
    def _():
        cbuf_ref[:, 0:SUBLANES, :] = jnp.zeros((CONV_DIM // LANES, SUBLANES, LANES), F32)

    main = lambda c: _dot(
        xn_ref[...], w_ref[:, COL_XBC + c * PROJ_COLS:COL_XBC + (c + 1) * PROJ_COLS])
    z_chunks = SSD_WIDTH // PROJ_COLS
    z_every = n_chunks // z_chunks
    dt_ref[...] = _dot(xn_ref[...], w_ref[:, COL_DT:SSD_COLS])
    pending = [main(c) for c in range(DENSE_LOOKAHEAD)]
    for c in range(n_chunks):
        res = pending.pop(0)
        if c + DENSE_LOOKAHEAD < n_chunks:
            pending.append(main(c + DENSE_LOOKAHEAD))
        if c % z_every == 0:
            zc = slice(COL_Z + (c // z_every) * PROJ_COLS, COL_Z + (c // z_every + 1) * PROJ_COLS)
            z_ref[:, zc] = _dot(xn_ref[...], w_ref[:, zc]).astype(BF16)
        for s in range(per):
            slab = c * per + s
            lanes = slice(slab * LANES, (slab + 1) * LANES)
            cbuf_ref[slab, SUBLANES:SUBLANES + tm, :] = res[:, s * LANES:(s + 1) * LANES]
            conv = cb_ref[:, lanes]
            for k in range(CONV_K):
                off = SUBLANES - (CONV_K - 1) + k
                conv = conv + cw_ref[k:k + 1, lanes] * cbuf_ref[slab, off:off + tm, :]
            cbuf_ref[slab, 0:SUBLANES, :] = cbuf_ref[slab, tm:tm + SUBLANES, :]
            act = conv + conv * jnp.tanh(conv)
            if slab * LANES < SSD_WIDTH:
                xs_ref[:, lanes] = act
            else:
                bc_ref[:, slab * LANES - SSD_WIDTH:(slab + 1) * LANES - SSD_WIDTH] = act.astype(BF16)


def _proj_ssd(x, gain, w, cw, cb, seq):
    t, d = x.shape
    row = lambda n: pl.BlockSpec((TM, n), lambda i: (i, 0))
    shp = lambda n, dt: jax.ShapeDtypeStruct((t, n), dt)
    return pl.pallas_call(
        functools.partial(_proj_ssd_body, tiles_per_seq=seq // TM), grid=(t // TM,),
        in_specs=[row(d), _const_spec((1, d)), _const_spec((d, SSD_COLS)),
                  _const_spec((CONV_K, CONV_DIM)), _const_spec((1, CONV_DIM))],
        out_specs=[row(SSD_WIDTH), row(SSD_WIDTH), row(BC_WIDTH), row(LANES)],
        out_shape=[shp(SSD_WIDTH, BF16), shp(SSD_WIDTH, F32), shp(BC_WIDTH, BF16),
                   shp(LANES, F32)],
        scratch_shapes=[pltpu.VMEM((CONV_DIM // LANES, SUBLANES + TM, LANES), F32),
                        pltpu.VMEM((TM, d), BF16)],
        compiler_params=_params(1), name="proj_ssd")(x, gain, w, cw, cb)


def _proj_att_body(x_ref, g_ref, w_ref, qg_ref, kg_ref, bd2_ref,
                   qn_ref, kn_ref, vn_ref, q16_ref, k16_ref, v16_ref, scr_ref, scr2_ref, xn_ref):
    tm = x_ref.shape[0]
    ni = tm // RES
    per = PROJ_COLS // LANES
    n_slabs = scr_ref.shape[0]
    xn_ref[...] = _rms(x_ref[...], g_ref[...]).astype(BF16)
    bd2 = bd2_ref[...]
    plan = ((qg_ref, qn_ref, q16_ref), (kg_ref, kn_ref, k16_ref), (None, vn_ref, v16_ref))
    chunks = [(t, c) for t in range(3) for c in range(ATT_WIDTH // PROJ_COLS)]
    main = lambda t, c: _dot(
        xn_ref[...], w_ref[:, t * ATT_WIDTH + c * PROJ_COLS:t * ATT_WIDTH + (c + 1) * PROJ_COLS])
    pending = [main(*ch) for ch in chunks[:DENSE_LOOKAHEAD]]
    for ci, (t, c) in enumerate(chunks):
        gain_ref, nat_ref, l16_ref = plan[t]
        res = pending.pop(0)
        if ci + DENSE_LOOKAHEAD < len(chunks):
            pending.append(main(*chunks[ci + DENSE_LOOKAHEAD]))
        if gain_ref is not None:
            ms_all = _dot((res * res).astype(BF16), bd2)
        for s in range(per):
            lanes = slice(c * PROJ_COLS + s * LANES, c * PROJ_COLS + (s + 1) * LANES)
            y = res[:, s * LANES:(s + 1) * LANES]
            if gain_ref is not None:
                y = y * lax.rsqrt(ms_all[:, s * LANES:(s + 1) * LANES] + EPS) * gain_ref[...]
            nat_ref[:, lanes] = y.astype(BF16)
            slab = (ci * per + s) % n_slabs
            scr_ref[slab] = y
            nj = tm // 4
            for r in range(4):
                scr2_ref[slab, r * nj:(r + 1) * nj, :] = scr_ref[slab, pl.ds(r, nj, stride=4), :]
            for r in range(4):
                for a in range(4):
                    l16_ref[4 * a + r, :, lanes] = (
                        scr2_ref[slab, pl.ds(r * nj + a, ni, stride=4), :].astype(BF16))


def _proj_att(x, gain, w, qg, kg, bd2, batch, seq):
    t, d = x.shape
    tps = seq // TM
    ni = TM // RES
    row = lambda n: pl.BlockSpec((TM, n), lambda i: (i, 0))
    l16 = pl.BlockSpec((None, RES, ni, ATT_WIDTH), lambda i: (i // tps, 0, i % tps, 0))
    nat_shape = jax.ShapeDtypeStruct((t, ATT_WIDTH), BF16)
    l16_shape = jax.ShapeDtypeStruct((batch, RES, seq // RES, ATT_WIDTH), BF16)
    return pl.pallas_call(
        _proj_att_body, grid=(t // TM,),
        in_specs=[row(d), _const_spec((1, d)), _const_spec((d, 3 * ATT_WIDTH)),
                  _const_spec((1, LANES)), _const_spec((1, LANES)),
                  _const_spec((PROJ_COLS, PROJ_COLS))],
        out_specs=[row(ATT_WIDTH)] * 3 + [l16] * 3,
        out_shape=[nat_shape] * 3 + [l16_shape] * 3,
        scratch_shapes=[pltpu.VMEM((2 * PROJ_COLS // LANES, TM, LANES), F32),
                        pltpu.VMEM((2 * PROJ_COLS // LANES, TM, LANES), F32),
                        pltpu.VMEM((TM, d), BF16)],
        compiler_params=_params(1), name="proj_att")(x, gain, w, qg, kg, bd2)


def _ssd_body(xs_ref, bc_ref, dt_ref, z_ref, dtb_ref, alog_ref, dskip_ref, nw_ref,
              tril_ref, eye_ref, ehead_ref, ecol_ref, y_ref, state_ref, acumt_ref, *, chunks):
    L = SSD_CHUNK
    gsz = SSD_GROUPS * SSD_STATE

    @pl.when(pl.program_id(1) == 0)
    def _():
        state_ref[...] = jnp.zeros_like(state_ref)

    lane = lax.broadcasted_iota(jnp.int32, (L, SSD_WIDTH), 1)
    even_head = (lane // SSD_HEAD_DIM) % 2 == 0
    heads_per_group = SSD_HEADS // SSD_GROUPS
    gw = heads_per_group * SSD_HEAD_DIM
    groups = range(SSD_GROUPS)

    def front(ci):
        rows = slice(ci * L, (ci + 1) * L)
        cgs = [bc_ref[rows, gsz + g * SSD_STATE:gsz + (g + 1) * SSD_STATE] for g in groups]
        bgs = [bc_ref[rows, g * SSD_STATE:(g + 1) * SSD_STATE] for g in groups]
        cbms = [_dot_nt(cgs[g], bgs[g]) for g in groups]
        dt = jax.nn.softplus(dt_ref[rows, :] + dtb_ref[...])
        adt = dt * (-jnp.exp(alog_ref[...]) * LOG2E)
        tril = tril_ref[...]
        hi, mid, lo = _split3(adt)
        acum = _dot(tril, hi) + _dot(tril, mid) + _dot(tril, lo)
        acumt_ref[ci] = acum.T
        dt_x = _dot(_split3_lanes(dt), ehead_ref[...])
        acum3 = _split3_lanes(acum)
        acum_x = _dot(acum3, ehead_ref[...])
        acum_col = _dot(jnp.concatenate([acum3, eye_ref[...]], axis=1), ecol_ref[...])
        return cgs, bgs, cbms, dt_x, acum_x, acum_col

    def back(ci, cgs, bgs, cbms, dt_x, acum_x, acum_col):
        rows = slice(ci * L, (ci + 1) * L)
        xs = xs_ref[rows, :]
        sts = [state_ref[:, g * gw:(g + 1) * gw] for g in groups]
        y_offs = [_dot(cgs[g], sts[g].astype(BF16)) for g in groups]
        xdt = xs * dt_x
        eacum_x = jnp.exp2(acum_x)
        tot_x = acum_x[L - 1:L, :]
        xdt_b = xdt.astype(BF16)
        zero = jnp.zeros_like(xdt_b)
        xdt_even = jnp.where(even_head, xdt_b, zero)
        xdt_odd = jnp.where(even_head, zero, xdt_b)
        xdec_b = (xdt * jnp.exp2(tot_x - acum_x)).astype(BF16)
        chunk_decay = eacum_x[L - 1:L, :]
        for g in groups:
            gs = slice(g * gw, (g + 1) * gw)
            new_states = _dot_tn(bgs[g], xdec_b[:, gs])
            state_ref[:, gs] = sts[g] * chunk_decay[:, gs] + new_states
        y_parts = []
        for g in groups:
            gs = slice(g * gw, (g + 1) * gw)
            ms = []
            for r in range(heads_per_group):
                h = g * heads_per_group + r
                seg = acum_col[:, h * L:(h + 1) * L] - acumt_ref[ci, h:h + 1, :]
                ms.append((cbms[g] * jnp.exp2(seg)).astype(BF16))
            yd = []
            for pr in range(heads_per_group // 2):
                pair = g * (heads_per_group // 2) + pr
                sl = slice(pair * LANES, (pair + 1) * LANES)
                lhs = jnp.concatenate([ms[2 * pr], ms[2 * pr + 1]], axis=1)
                rhs = jnp.concatenate([xdt_even[:, sl], xdt_odd[:, sl]], axis=0)
                yd.append(_dot(lhs, rhs))
            y_parts.append(jnp.concatenate(yd, axis=1) + y_offs[g] * eacum_x[:, gs])

        y = jnp.concatenate(y_parts, axis=1) + xs * dskip_ref[...]
        y = y * _silu(z_ref[rows, :]).astype(F32)
        outs = []
        for g in groups:
            yg = y[:, g * gw:(g + 1) * gw]
            msq = jnp.mean(yg * yg, axis=-1, keepdims=True)
            outs.append(yg * lax.rsqrt(msq + EPS))
        y_ref[rows, :] = (jnp.concatenate(outs, axis=1) * nw_ref[...]).astype(BF16)

    pending = front(0)
    for ci in range(chunks):
        ready = pending
        if ci + 1 < chunks:
            pending = front(ci + 1)
        back(ci, *ready)


def _ssd(xs, bc, dt, z, dtb, alog, dskip_x, nw, tril, eye, ehead, ecol, batch, seq):
    L = SSD_CHUNK
    ns = seq // SSD_ROWS
    chunks = SSD_ROWS // L
    blk = lambda n: pl.BlockSpec((SSD_ROWS, n), lambda b, s: (b * ns + s, 0))
    return pl.pallas_call(
        functools.partial(_ssd_body, chunks=chunks), grid=(batch, ns),
        in_specs=[blk(SSD_WIDTH), blk(BC_WIDTH), blk(LANES), blk(SSD_WIDTH),
                  _const_spec((1, LANES)), _const_spec((1, LANES)),
                  _const_spec((1, SSD_WIDTH)), _const_spec((1, SSD_WIDTH)),
                  _const_spec((L, L)), _const_spec((L, L)), _const_spec((LANES, SSD_WIDTH)),
                  _const_spec((LANES + L, SSD_HEADS * L))],
        out_specs=blk(SSD_WIDTH),
        out_shape=jax.ShapeDtypeStruct((batch * seq, SSD_WIDTH), BF16),
        scratch_shapes=[pltpu.VMEM((SSD_STATE, SSD_WIDTH), F32),
                        pltpu.VMEM((chunks, L, LANES), F32)],
        compiler_params=_params(2), name="ssd")(
            xs, bc, dt, z, dtb, alog, dskip_x, nw, tril, eye, ehead, ecol)


def _attn_units(units, stat_ref, st_ref):
    blk = BAND_BLOCK
    lane = lax.broadcasted_iota(jnp.int32, (blk, LANES), 1)
    first_head = lane < ATT_HEAD_DIM
    srow = lax.broadcasted_iota(jnp.int32, (LANES, blk), 0)
    first_rows = srow < ATT_HEAD_DIM
    stat_ref[...] = jnp.zeros_like(stat_ref)

    def scores(unit):
        q2 = unit["q"]()
        zero = jnp.zeros_like(q2)
        qst = jnp.concatenate([jnp.where(first_head, q2, zero),
                               jnp.where(first_head, zero, q2)], axis=0)
        return _dot_nt(unit["k"](), qst)

    n_slots = st_ref.shape[0]
    lookahead = n_slots - 1

    def issue(idx):
        st_ref[idx % n_slots] = scores(units[idx])

    for idx in range(min(lookahead, len(units))):
        issue(idx)
    for idx, unit in enumerate(units):
        if idx + lookahead < len(units):
            issue(idx + lookahead)
        bias = unit["bias"]()
        ps, mxs = [], []
        for hh in range(2):
            s = st_ref[idx % n_slots, :, hh * blk:(hh + 1) * blk] + bias
            mx = jnp.max(s, axis=0, keepdims=True)
            ps.append(jnp.exp2(s - mx).astype(BF16))
            mxs.append(mx)
        vt = unit["vt"]()
        ones = jnp.ones((DEN_ROWS, vt.shape[1]), BF16)
        ot = _dot(jnp.concatenate([vt, ones], axis=0), jnp.concatenate(ps, axis=1))
        outs = []
        for hh in range(2):
            den = ot[LANES:LANES + 1, hh * blk:(hh + 1) * blk]
            outs.append(ot[0:LANES, hh * blk:(hh + 1) * blk] * (1.0 / den))
            lse = (mxs[hh] + jnp.log2(den)) * LN2
            for rep in range(HEAD_REP):
                row = rep * ATT_HEADS + 2 * unit["pair"] + hh
                stat_ref[unit["slot"], row:row + 1, :] = lse
        o_t = jnp.where(first_rows, outs[0], outs[1])
        unit["write"](o_t.T.astype(BF16))


def _attn_d1_body(q_ref, kc_ref, kp_ref, vc_ref, vp_ref, bias_ref, o_ref, lse_ref, stat_ref,
                  st_ref):
    blk = BAND_BLOCK
    n_sub = ATT_ROWS // blk
    has_prev = jnp.where(pl.program_id(1) > 0, 1, 0)
    units = []
    for p in range(N_PAIRS):
        sl = slice(p * LANES, (p + 1) * LANES)
        vts = {}

        def vt_block(m, sl=sl, vts=vts):
            if m not in vts:
                v = vp_ref[:, sl] if m == 0 else vc_ref[(m - 1) * blk:m * blk, sl]
                vts[m] = v.T
            return vts[m]

        for j in range(n_sub):
            def write(o, j=j, sl=sl):
                o_ref[j * blk:(j + 1) * blk, sl] = o

            if j == 0:
                k_of = lambda sl=sl: jnp.concatenate([kp_ref[:, sl], kc_ref[0:blk, sl]], axis=0)
                bias_of = lambda: bias_ref[has_prev]
            else:
                k_of = lambda j=j, sl=sl: kc_ref[(j - 1) * blk:(j + 1) * blk, sl]
                bias_of = lambda: bias_ref[1]
            units.append(dict(
                q=lambda j=j, sl=sl: q_ref[j * blk:(j + 1) * blk, sl], k=k_of,
                vt=lambda j=j, f=vt_block: jnp.concatenate([f(j), f(j + 1)], axis=1),
                bias=bias_of, slot=j, pair=p, write=write))
    _attn_units(units, stat_ref, st_ref)
    for j in range(n_sub):
        lse_ref[j * blk:(j + 1) * blk, :] = stat_ref[j].T


def _attn_d4_body(q_ref, kc_ref, kp_ref, vc_ref, vp_ref, bias_ref, o_ref, lse_ref, stat_ref,
                  st_ref):
    blk = BAND_BLOCK
    ni = blk // 4
    n_groups = q_ref.shape[1] // ni
    has_prev = jnp.where(pl.program_id(1) > 0, 1, 0)
    rows = lambda g: slice(g * ni, (g + 1) * ni)
    gather = lambda ref, r, g, sl: [ref[4 * a + r, rows(g), sl] for a in range(4)]

    def window(cur_ref, prev_ref, r, g, sl):
        before = gather(prev_ref, r, 0, sl) if g == 0 else gather(cur_ref, r, g - 1, sl)
        return jnp.concatenate(before, axis=0), jnp.concatenate(gather(cur_ref, r, g, sl), axis=0)

    units = []
    for p in range(N_PAIRS):
        sl = slice(p * LANES, (p + 1) * LANES)
        for g in range(n_groups):
            for r in range(4):
                def write(o, r=r, g=g, sl=sl):
                    for a in range(4):
                        o_ref[4 * a + r, rows(g), sl] = o[a * ni:(a + 1) * ni, :]

                units.append(dict(
                    q=lambda r=r, g=g, sl=sl: jnp.concatenate(gather(q_ref, r, g, sl), axis=0),
                    k=lambda r=r, g=g, sl=sl: jnp.concatenate(
                        window(kc_ref, kp_ref, r, g, sl), axis=0),
                    vt=lambda r=r, g=g, sl=sl: jnp.concatenate(
                        [v.T for v in window(vc_ref, vp_ref, r, g, sl)], axis=1),
                    bias=(lambda: bias_ref[has_prev]) if g == 0 else (lambda: bias_ref[1]),
                    slot=g * 4 + r, pair=p, write=write))
    _attn_units(units, stat_ref, st_ref)
    for g in range(n_groups):
        for r in range(4):
            lt = stat_ref[g * 4 + r].T
            for a in range(4):
                lse_ref[4 * a + r, rows(g), :] = lt[a * ni:(a + 1) * ni, :]


def _attn_d16_body(q_ref, k_ref, v_ref, bias_ref, o_ref, lse_ref, stat_ref, st_ref):
    n_sub = q_ref.shape[0]
    units = []
    for p in range(N_PAIRS):
        sl = slice(p * LANES, (p + 1) * LANES)
        for j in range(n_sub):
            def write(o, j=j, sl=sl):
                o_ref[j, :, sl] = o

            units.append(dict(
                q=lambda j=j, sl=sl: q_ref[j, :, sl], k=lambda j=j, sl=sl: k_ref[j, :, sl],
                vt=lambda j=j, sl=sl: v_ref[j, :, sl].T,
                bias=lambda: bias_ref[0], slot=j, pair=p, write=write))
    _attn_units(units, stat_ref, st_ref)
    for j in range(n_sub):
        lse_ref[j] = stat_ref[j].T


def _attn_scratch(n_sub, nk, lookahead=SCORE_LOOKAHEAD):
    return [pltpu.VMEM((n_sub, LANES, BAND_BLOCK), F32),
            pltpu.VMEM((lookahead + 1, nk, 2 * BAND_BLOCK), F32)]


def _band_bias(kk, qi, need_prev):
    ok = (kk[:, None] >= qi[None, :]) & (kk[:, None] <= qi[None, :] + BAND_BLOCK)
    if need_prev is not None:
        ok = ok & need_prev[:, None]
    return np.where(ok, 0.0, -np.inf).astype(np.float32)


def _attn_d1(q, k, v, batch, seq):
    blk = BAND_BLOCK
    w = ATT_WIDTH
    nt = seq // ATT_ROWS
    per = ATT_ROWS // blk
    kk = np.arange(2 * blk)
    qi = np.arange(blk)
    bias = jnp.asarray(np.stack([_band_bias(kk, qi, kk >= blk), _band_bias(kk, qi, None)]))
    cur = lambda n: pl.BlockSpec((ATT_ROWS, n), lambda b, i: (b * nt + i, 0))
    prev = pl.BlockSpec((blk, w), lambda b, i: (b * nt * per + jnp.maximum(i * per - 1, 0), 0))
    return pl.pallas_call(
        _attn_d1_body, grid=(batch, nt),
        in_specs=[cur(w), cur(w), prev, cur(w), prev, _const_spec(bias.shape)],
        out_specs=[cur(w), cur(LANES)],
        out_shape=[jax.ShapeDtypeStruct((batch * seq, w), BF16),
                   jax.ShapeDtypeStruct((batch * seq, LANES), F32)],
        scratch_shapes=_attn_scratch(per, 2 * blk, SCORE_LOOKAHEAD_D1),
        compiler_params=_params(2), name="attn_d1")(q, k, k, v, v, bias)


def _attn_d4(q16, k16, v16, batch, seq):
    blk = BAND_BLOCK
    w = ATT_WIDTH
    ni = blk // 4
    nb = seq // 4 // blk
    pos = np.arange(blk)
    true_i = 4 * (pos % ni) + pos // ni
    kk = np.concatenate([true_i, true_i + blk])
    bias = jnp.asarray(np.stack([_band_bias(kk, true_i, kk >= blk), _band_bias(kk, true_i, None)]))
    per = ATT_ROWS // (4 * blk)
    cur = lambda n: pl.BlockSpec((None, RES, per * ni, n), lambda b, i: (b, 0, i, 0))
    prev = pl.BlockSpec((None, RES, ni, w), lambda b, i: (b, 0, jnp.maximum(i * per - 1, 0), 0))
    return pl.pallas_call(
        _attn_d4_body, grid=(batch, nb // per),
        in_specs=[cur(w), cur(w), prev, cur(w), prev, _const_spec(bias.shape)],
        out_specs=[cur(w), cur(LANES)],
        out_shape=[jax.ShapeDtypeStruct((batch, RES, seq // RES, w), BF16),
                   jax.ShapeDtypeStruct((batch, RES, seq // RES, LANES), F32)],
        scratch_shapes=_attn_scratch(4 * per, 2 * blk),
        compiler_params=_params(2), name="attn_d4")(q16, k16, k16, v16, v16, bias)


def _attn_d16(q16, k16, v16, batch, seq):
    blk = BAND_BLOCK
    w = ATT_WIDTH
    n_sub = ATT_ROWS // blk
    assert seq // RES == blk and RES % n_sub == 0
    kk = np.arange(blk)
    bias = jnp.asarray(_band_bias(kk + blk, kk, None)[None])
    spec = lambda n: pl.BlockSpec((None, n_sub, blk, n), lambda b, g: (b, g, 0, 0))
    return pl.pallas_call(
        _attn_d16_body, grid=(batch, RES // n_sub),
        in_specs=[spec(w), spec(w), spec(w), _const_spec(bias.shape)],
        out_specs=[spec(w), spec(LANES)],
        out_shape=[jax.ShapeDtypeStruct((batch, RES, blk, w), BF16),
                   jax.ShapeDtypeStruct((batch, RES, blk, LANES), F32)],
        scratch_shapes=_attn_scratch(n_sub, blk),
        compiler_params=_params(2), name="attn_d16")(q16, k16, v16, bias)


def _outproj_body(x_ref, ys_ref, o1_ref, l1_ref, o4_ref, l4_ref, o16_ref, l16_ref, ehead_ref,
                  w_ref, out_ref, small_ref, perm_ref, perm2_ref):
    tm = x_ref.shape[0]
    ni = tm // RES

    def to_natural(l_ref):
        for r in range(RES):
            small_ref[pl.ds(r, ni, stride=RES), :] = l_ref[r]
        return small_ref[...]

    def to_l16(val):
        small_ref[...] = val
        return jnp.concatenate([small_ref[pl.ds(r, ni, stride=RES), :] for r in range(RES)], axis=0)

    l1 = l1_ref[...]
    l4 = to_natural(l4_ref)
    l16 = to_natural(l16_ref)
    m = jnp.maximum(jnp.maximum(l1, l4), l16)
    e1, e4, e16 = jnp.exp(l1 - m), jnp.exp(l4 - m), jnp.exp(l16 - m)
    inv = 1.0 / (e1 + e4 + e16)
    a1 = _split3_lanes(e1 * inv)
    a4 = _split3_lanes(to_l16(e4 * inv))
    a16 = _split3_lanes(to_l16(e16 * inv))
    acc = x_ref[...] + _dot(ys_ref[...], w_ref[0:SSD_WIDTH, :])
    per = MXU_COLS // LANES
    for c in range(ATT_WIDTH // MXU_COLS):
        cols = slice(c * MXU_COLS, (c + 1) * MXU_COLS)
        ehead = ehead_ref[:, cols]
        o4 = o4_ref[:, :, cols].reshape(tm, MXU_COLS).astype(F32)
        o16 = o16_ref[:, :, cols].reshape(tm, MXU_COLS).astype(F32)
        y_l16 = _dot(a4, ehead) * o4 + _dot(a16, ehead) * o16
        nj = tm // 4
        for s in range(per):
            slab = c * per + s
            for r in range(4):
                for a in range(4):
                    perm2_ref[slab, pl.ds(r * nj + a, ni, stride=4), :] = (
                        y_l16[(4 * a + r) * ni:(4 * a + r + 1) * ni, s * LANES:(s + 1) * LANES])
            for r in range(4):
                perm_ref[slab, pl.ds(r, nj, stride=4), :] = perm2_ref[slab, r * nj:(r + 1) * nj, :]
        y_nat = jnp.concatenate([perm_ref[c * per + s] for s in range(per)], axis=1)
        y_att = _dot(a1, ehead) * o1_ref[:, cols].astype(F32) + y_nat
        acc = acc + _dot(y_att.astype(BF16),
                         w_ref[SSD_WIDTH + c * MXU_COLS:SSD_WIDTH + (c + 1) * MXU_COLS, :])
    out_ref[...] = acc


def _outproj(x, y_ssd, o1, l1, o4, l4, o16, l16, ehead, w, seq):
    t, d = x.shape
    tps = seq // TM
    ni = TM // RES
    row = lambda n: pl.BlockSpec((TM, n), lambda i: (i, 0))
    slab = lambda n: pl.BlockSpec((None, RES, ni, n), lambda i: (i // tps, 0, i % tps, 0))
    return pl.pallas_call(
        _outproj_body, grid=(t // TM,),
        in_specs=[row(d), row(SSD_WIDTH), row(ATT_WIDTH), row(LANES), slab(ATT_WIDTH), slab(LANES),
                  slab(ATT_WIDTH), slab(LANES), _const_spec((LANES, ATT_WIDTH)),
                  _const_spec((SSD_WIDTH + ATT_WIDTH, d))],
        out_specs=row(d), out_shape=jax.ShapeDtypeStruct((t, d), F32),
        scratch_shapes=[pltpu.VMEM((TM, LANES), F32),
                        pltpu.VMEM((ATT_WIDTH // LANES, TM, LANES), F32),
                        pltpu.VMEM((ATT_WIDTH // LANES, TM, LANES), F32)],
        compiler_params=_params(1), name="outproj")(
            x, y_ssd, o1, l1, o4, l4, o16, l16, ehead, w)


def _head_expand(width_per_head):
    r = jnp.arange(LANES)[:, None]
    c = jnp.arange(SSD_HEADS * width_per_head)[None, :]
    return ((r < HEAD_REP * SSD_HEADS) & (r % SSD_HEADS == c // width_per_head)).astype(BF16)


def _rep_heads(v):
    return jnp.pad(jnp.tile(v.astype(F32), HEAD_REP), (0, LANES - HEAD_REP * SSD_HEADS))[None, :]


def kernel(x, ffn1_norm, ffn1_w_gate, ffn1_w_up, ffn1_w_down, mix_norm, w_in, conv_w, conv_b,
           dt_bias, a_log, d_skip, ssd_norm, q_norm, k_norm, w_out, ffn2_norm, ffn2_w_gate,
           ffn2_w_up, ffn2_w_down):
    batch, seq, d = x.shape
    depth = w_in.shape[0]
    assert d == D_MODEL and seq == RES * BAND_BLOCK
    assert all(wd // dl == BAND_BLOCK for wd, dl in ATT_BRANCHES)
    assert seq % TM == 0 and seq % ATT_ROWS == 0 and seq % SSD_ROWS == 0
    t = batch * seq

    tril = jnp.tril(jnp.ones((SSD_CHUNK, SSD_CHUNK), BF16))
    eye = jnp.eye(SSD_CHUNK, dtype=BF16)
    ehead = _head_expand(SSD_HEAD_DIM)
    masked = jnp.where(tril > 0, 0.0, MASKED).astype(BF16)
    ecol = jnp.concatenate([_head_expand(SSD_CHUNK), jnp.tile(masked, (1, SSD_HEADS))], axis=0)
    half = jnp.arange(PROJ_COLS) // ATT_HEAD_DIM
    bd2 = ((half[:, None] == half[None, :]).astype(F32) / ATT_HEAD_DIM).astype(BF16)
    scale = LOG2E / math.sqrt(ATT_HEAD_DIM)

    xf = x.reshape(t, d)
    for i in range(depth):
        dt_cols = jnp.pad(jnp.tile(w_in[i][:, COL_DT:COL_DT + SSD_HEADS], (1, HEAD_REP)),
                          ((0, 0), (0, LANES - HEAD_REP * SSD_HEADS)))
        w_ssd = jnp.concatenate([w_in[i][:, :COL_DT], dt_cols], axis=1).astype(BF16)
        w_att = w_in[i][:, COL_DT + SSD_HEADS:].astype(BF16)
        qg = jnp.tile(q_norm[i].astype(F32) * scale, LANES // ATT_HEAD_DIM)[None, :]
        kg = jnp.tile(k_norm[i].astype(F32), LANES // ATT_HEAD_DIM)[None, :]

        xf = _ffn(xf, ffn1_norm[i][None, :], ffn1_w_gate[i].astype(BF16),
                  ffn1_w_up[i].astype(BF16), ffn1_w_down[i].astype(BF16))
        z, xs, bc, dt = _proj_ssd(xf, mix_norm[i][None, :], w_ssd, 0.5 * conv_w[i],
                                  0.5 * conv_b[i][None, :], seq)
        qn, kn, vn, q16, k16, v16 = _proj_att(xf, mix_norm[i][None, :], w_att, qg, kg, bd2,
                                              batch, seq)
        y_ssd = _ssd(xs, bc, dt, z, _rep_heads(dt_bias[i]), _rep_heads(a_log[i]),
                     jnp.repeat(d_skip[i].astype(F32), SSD_HEAD_DIM)[None, :],
                     ssd_norm[i][None, :], tril, eye, ehead, ecol, batch, seq)
        o1, l1 = _attn_d1(qn, kn, vn, batch, seq)
        o4, l4 = _attn_d4(q16, k16, v16, batch, seq)
        o16, l16 = _attn_d16(q16, k16, v16, batch, seq)
        xf = _outproj(xf, y_ssd, o1, l1, o4, l4, o16, l16, ehead, w_out[i].astype(BF16), seq)
        xf = _ffn(xf, ffn2_norm[i][None, :], ffn2_w_gate[i].astype(BF16),
                  ffn2_w_up[i].astype(BF16), ffn2_w_down[i].astype(BF16))
    return xf.reshape(batch, seq, d)
```

```python
import functools
import math

import numpy as np
import jax
import jax.numpy as jnp
from jax import lax
from jax.experimental import pallas as pl
from jax.experimental.pallas import tpu as pltpu

F32 = jnp.float32
BF16 = jnp.bfloat16

D_MODEL = 1024
SSD_HEADS = 16
SSD_HEAD_DIM = 64
SSD_WIDTH = SSD_HEADS * SSD_HEAD_DIM
SSD_GROUPS = 4
SSD_STATE = 128
CONV_K = 4
SSD_CHUNK = 128
BC_WIDTH = 2 * SSD_GROUPS * SSD_STATE
CONV_DIM = SSD_WIDTH + BC_WIDTH
ATT_HEADS = 16
ATT_HEAD_DIM = 64
ATT_WIDTH = ATT_HEADS * ATT_HEAD_DIM
ATT_BRANCHES = ((128, 1), (512, 4), (2048, 16))
BAND_BLOCK = 128
EPS = 1e-6

LANES = 128
SUBLANES = 8
MXU_COLS = 256
PROJ_COLS = 256
VMEM_LIMIT_BYTES = 56 * 1024 * 1024

LOG2E = math.log2(math.e)
LN2 = math.log(2.0)
MASKED = -1e30
DEN_ROWS = 16
SCORE_LOOKAHEAD = 6
SCORE_LOOKAHEAD_D1 = 10
DENSE_LOOKAHEAD = 2

HEAD_REP = 3
RES = 16
N_PAIRS = ATT_HEADS // 2

COL_Z = 0
COL_XBC = COL_Z + SSD_WIDTH
COL_DT = COL_XBC + CONV_DIM
SSD_COLS = COL_DT + LANES

TM = 1024
ATT_ROWS = 2048
SSD_ROWS = 1024

def _const_spec(shape):
    nd = len(shape)
    return pl.BlockSpec(shape, lambda *_: (0,) * nd, pipeline_mode=pl.Buffered(1))


def _params(n_axes):
    return pltpu.CompilerParams(dimension_semantics=("arbitrary",) * n_axes,
                                vmem_limit_bytes=VMEM_LIMIT_BYTES)


def _rms(x, gain):
    ms = jnp.mean(x * x, axis=-1, keepdims=True)
    return x * lax.rsqrt(ms + EPS) * gain


def _silu(x):
    h = 0.5 * x
    return h + h * jnp.tanh(h)


def _dot(a, b):
    return jnp.dot(a, b, preferred_element_type=F32)


def _dot_nt(a, b):
    return lax.dot_general(a, b, (((1,), (1,)), ((), ())), preferred_element_type=F32)


def _dot_tn(a, b):
    return lax.dot_general(a, b, (((0,), (0,)), ((), ())), preferred_element_type=F32)


def _split3(v):
    hi = v.astype(BF16)
    r1 = v - hi.astype(F32)
    mid = r1.astype(BF16)
    lo = (r1 - mid.astype(F32)).astype(BF16)
    return hi, mid, lo


def _split3_lanes(v):
    hi, mid, lo = _split3(v)
    lane = lax.broadcasted_iota(jnp.int32, v.shape, 1)
    zero = jnp.zeros_like(hi)
    return jnp.where(lane < SSD_HEADS, hi,
                     jnp.where(lane < 2 * SSD_HEADS, mid,
                               jnp.where(lane < 3 * SSD_HEADS, lo, zero)))


def _ffn_body(x_ref, g_ref, wg_ref, wu_ref, wd_ref, o_ref):
    x = x_ref[...]
    xn = _rms(x, g_ref[...]).astype(BF16)
    gate = _dot(xn, wg_ref[...])
    up = _dot(xn, wu_ref[...])
    h = (_silu(gate) * up).astype(BF16)
    o_ref[...] = x + 0.5 * _dot(h, wd_ref[...])


def _ffn(x, gain, wg, wu, wd):
    t, d = x.shape
    f = wg.shape[1]
    row = pl.BlockSpec((TM, d), lambda i: (i, 0))
    return pl.pallas_call(
        _ffn_body, grid=(t // TM,),
        in_specs=[row, _const_spec((1, d)), _const_spec((d, f)), _const_spec((d, f)),
                  _const_spec((f, d))],
        out_specs=row, out_shape=jax.ShapeDtypeStruct((t, d), F32),
        compiler_params=_params(1), name="ffn")(x, gain, wg, wu, wd)


def _proj_ssd_body(x_ref, g_ref, w_ref, cw_ref, cb_ref, z_ref, xs_ref, bc_ref, dt_ref, cbuf_ref,
                   xn_ref, *, tiles_per_seq):
    tm = x_ref.shape[0]
    n_chunks = CONV_DIM // PROJ_COLS
    per = PROJ_COLS // LANES
    xn_ref[...] = _rms(x_ref[...], g_ref[...]).astype(BF16)

    @pl.when(pl.program_id(0) % tiles_per_seq == 0)
    def _():
        cbuf_ref[:, 0:SUBLANES, :] = jnp.zeros((CONV_DIM // LANES, SUBLANES, LANES), F32)

    main = lambda c: _dot(
        xn_ref[...], w_ref[:, COL_XBC + c * PROJ_COLS:COL_XBC + (c + 1) * PROJ_COLS])
    z_chunks = SSD_WIDTH // PROJ_COLS
    z_every = n_chunks // z_chunks
    dt_ref[...] = _dot(xn_ref[...], w_ref[:, COL_DT:SSD_COLS])
    pending = [main(c) for c in range(DENSE_LOOKAHEAD)]
    for c in range(n_chunks):
        res = pending.pop(0)
        if c + DENSE_LOOKAHEAD < n_chunks:
            pending.append(main(c + DENSE_LOOKAHEAD))
        if c % z_every == 0:
            zc = slice(COL_Z + (c // z_every) * PROJ_COLS, COL_Z + (c // z_every + 1) * PROJ_COLS)
            z_ref[:, zc] = _dot(xn_ref[...], w_ref[:, zc]).astype(BF16)
        for s in range(per):
            slab = c * per + s
            lanes = slice(slab * LANES, (slab + 1) * LANES)
            cbuf_ref[slab, SUBLANES:SUBLANES + tm, :] = res[:, s * LANES:(s + 1) * LANES]
            conv = cb_ref[:, lanes]
            for k in range(CONV_K):
                off = SUBLANES - (CONV_K - 1) + k
                conv = conv + cw_ref[k:k + 1, lanes] * cbuf_ref[slab, off:off + tm, :]
            cbuf_ref[slab, 0:SUBLANES, :] = cbuf_ref[slab, tm:tm + SUBLANES, :]
            act = conv + conv * jnp.tanh(conv)
            if slab * LANES < SSD_WIDTH:
                xs_ref[:, lanes] = act
            else:
                bc_ref[:, slab * LANES - SSD_WIDTH:(slab + 1) * LANES - SSD_WIDTH] = act.astype(BF16)


def _proj_ssd(x, gain, w, cw, cb, seq):
    t, d = x.shape
    row = lambda n: pl.BlockSpec((TM, n), lambda i: (i, 0))
    shp = lambda n, dt: jax.ShapeDtypeStruct((t, n), dt)
    return pl.pallas_call(
        functools.partial(_proj_ssd_body, tiles_per_seq=seq // TM), grid=(t // TM,),
        in_specs=[row(d), _const_spec((1, d)), _const_spec((d, SSD_COLS)),
                  _const_spec((CONV_K, CONV_DIM)), _const_spec((1, CONV_DIM))],
        out_specs=[row(SSD_WIDTH), row(SSD_WIDTH), row(BC_WIDTH), row(LANES)],
        out_shape=[shp(SSD_WIDTH, BF16), shp(SSD_WIDTH, F32), shp(BC_WIDTH, BF16),
                   shp(LANES, F32)],
        scratch_shapes=[pltpu.VMEM((CONV_DIM // LANES, SUBLANES + TM, LANES), F32),
                        pltpu.VMEM((TM, d), BF16)],
        compiler_params=_params(1), name="proj_ssd")(x, gain, w, cw, cb)


def _proj_att_body(x_ref, g_ref, w_ref, qg_ref, kg_ref, bd2_ref,
                   qn_ref, kn_ref, vn_ref, q16_ref, k16_ref, v16_ref, scr_ref, scr2_ref, xn_ref):
    tm = x_ref.shape[0]
    ni = tm // RES
    per = PROJ_COLS // LANES
    n_slabs = scr_ref.shape[0]
    xn_ref[...] = _rms(x_ref[...], g_ref[...]).astype(BF16)
    bd2 = bd2_ref[...]
    plan = ((qg_ref, qn_ref, q16_ref), (kg_ref, kn_ref, k16_ref), (None, vn_ref, v16_ref))
    chunks = [(t, c) for t in range(3) for c in range(ATT_WIDTH // PROJ_COLS)]
    main = lambda t, c: _dot(
        xn_ref[...], w_ref[:, t * ATT_WIDTH + c * PROJ_COLS:t * ATT_WIDTH + (c + 1) * PROJ_COLS])
    pending = [main(*ch) for ch in chunks[:DENSE_LOOKAHEAD]]
    for ci, (t, c) in enumerate(chunks):
        gain_ref, nat_ref, l16_ref = plan[t]
        res = pending.pop(0)
        if ci + DENSE_LOOKAHEAD < len(chunks):
            pending.append(main(*chunks[ci + DENSE_LOOKAHEAD]))
        if gain_ref is not None:
            ms_all = _dot((res * res).astype(BF16), bd2)
        for s in range(per):
            lanes = slice(c * PROJ_COLS + s * LANES, c * PROJ_COLS + (s + 1) * LANES)
            y = res[:, s * LANES:(s + 1) * LANES]
            if gain_ref is not None:
                y = y * lax.rsqrt(ms_all[:, s * LANES:(s + 1) * LANES] + EPS) * gain_ref[...]
            nat_ref[:, lanes] = y.astype(BF16)
            slab = (ci * per + s) % n_slabs
            scr_ref[slab] = y
            nj = tm // 4
            for r in range(4):
                scr2_ref[slab, r * nj:(r + 1) * nj, :] = scr_ref[slab, pl.ds(r, nj, stride=4), :]
            for r in range(4):
                for a in range(4):
                    l16_ref[4 * a + r, :, lanes] = (
                        scr2_ref[slab, pl.ds(r * nj + a, ni, stride=4), :].astype(BF16))


def _proj_att(x, gain, w, qg, kg, bd2, batch, seq):
    t, d = x.shape
    tps = seq // TM
    ni = TM // RES
    row = lambda n: pl.BlockSpec((TM, n), lambda i: (i, 0))
    l16 = pl.BlockSpec((None, RES, ni, ATT_WIDTH), lambda i: (i // tps, 0, i % tps, 0))
    nat_shape = jax.ShapeDtypeStruct((t, ATT_WIDTH), BF16)
    l16_shape = jax.ShapeDtypeStruct((batch, RES, seq // RES, ATT_WIDTH), BF16)
    return pl.pallas_call(
        _proj_att_body, grid=(t // TM,),
        in_specs=[row(d), _const_spec((1, d)), _const_spec((d, 3 * ATT_WIDTH)),
                  _const_spec((1, LANES)), _const_spec((1, LANES)),
                  _const_spec((PROJ_COLS, PROJ_COLS))],
        out_specs=[row(ATT_WIDTH)] * 3 + [l16] * 3,
        out_shape=[nat_shape] * 3 + [l16_shape] * 3,
        scratch_shapes=[pltpu.VMEM((2 * PROJ_COLS // LANES, TM, LANES), F32),
                        pltpu.VMEM((2 * PROJ_COLS // LANES, TM, LANES), F32),
                        pltpu.VMEM((TM, d), BF16)],
        compiler_params=_params(1), name="proj_att")(x, gain, w, qg, kg, bd2)


def _ssd_body(xs_ref, bc_ref, dt_ref, z_ref, dtb_ref, alog_ref, dskip_ref, nw_ref,
              tril_ref, eye_ref, ehead_ref, ecol_ref, y_ref, state_ref, acumt_ref, *, chunks):
    L = SSD_CHUNK
    gsz = SSD_GROUPS * SSD_STATE

    @pl.when(pl.program_id(1) == 0)
    def _():
        state_ref[...] = jnp.zeros_like(state_ref)

    lane = lax.broadcasted_iota(jnp.int32, (L, SSD_WIDTH), 1)
    even_head = (lane // SSD_HEAD_DIM) % 2 == 0
    heads_per_group = SSD_HEADS // SSD_GROUPS
    gw = heads_per_group * SSD_HEAD_DIM
    groups = range(SSD_GROUPS)

    def front(ci):
        rows = slice(ci * L, (ci + 1) * L)
        cgs = [bc_ref[rows, gsz + g * SSD_STATE:gsz + (g + 1) * SSD_STATE] for g in groups]
        bgs = [bc_ref[rows, g * SSD_STATE:(g + 1) * SSD_STATE] for g in groups]
        cbms = [_dot_nt(cgs[g], bgs[g]) for g in groups]
        dt = jax.nn.softplus(dt_ref[rows, :] + dtb_ref[...])
        adt = dt * (-jnp.exp(alog_ref[...]) * LOG2E)
        tril = tril_ref[...]
        hi, mid, lo = _split3(adt)
        acum = _dot(tril, hi) + _dot(tril, mid) + _dot(tril, lo)
        acumt_ref[ci] = acum.T
        dt_x = _dot(_split3_lanes(dt), ehead_ref[...])
        acum3 = _split3_lanes(acum)
        acum_x = _dot(acum3, ehead_ref[...])
        acum_col = _dot(jnp.concatenate([acum3, eye_ref[...]], axis=1), ecol_ref[...])
        return cgs, bgs, cbms, dt_x, acum_x, acum_col

    def back(ci, cgs, bgs, cbms, dt_x, acum_x, acum_col):
        rows = slice(ci * L, (ci + 1) * L)
        xs = xs_ref[rows, :]
        sts = [state_ref[:, g * gw:(g + 1) * gw] for g in groups]
        y_offs = [_dot(cgs[g], sts[g].astype(BF16)) for g in groups]
        xdt = xs * dt_x
        eacum_x = jnp.exp2(acum_x)
        tot_x = acum_x[L - 1:L, :]
        xdt_b = xdt.astype(BF16)
        zero = jnp.zeros_like(xdt_b)
        xdt_even = jnp.where(even_head, xdt_b, zero)
        xdt_odd = jnp.where(even_head, zero, xdt_b)
        xdec_b = (xdt * jnp.exp2(tot_x - acum_x)).astype(BF16)
        chunk_decay = eacum_x[L - 1:L, :]
        for g in groups:
            gs = slice(g * gw, (g + 1) * gw)
            new_states = _dot_tn(bgs[g], xdec_b[:, gs])
            state_ref[:, gs] = sts[g] * chunk_decay[:, gs] + new_states
        y_parts = []
        for g in groups:
            gs = slice(g * gw, (g + 1) * gw)
            ms = []
            for r in range(heads_per_group):
                h = g * heads_per_group + r
                seg = acum_col[:, h * L:(h + 1) * L] - acumt_ref[ci, h:h + 1, :]
                ms.append((cbms[g] * jnp.exp2(seg)).astype(BF16))
            yd = []
            for pr in range(heads_per_group // 2):
                pair = g * (heads_per_group // 2) + pr
                sl = slice(pair * LANES, (pair + 1) * LANES)
                lhs = jnp.concatenate([ms[2 * pr], ms[2 * pr + 1]], axis=1)
                rhs = jnp.concatenate([xdt_even[:, sl], xdt_odd[:, sl]], axis=0)
                yd.append(_dot(lhs, rhs))
            y_parts.append(jnp.concatenate(yd, axis=1) + y_offs[g] * eacum_x[:, gs])

        y = jnp.concatenate(y_parts, axis=1) + xs * dskip_ref[...]
        y = y * _silu(z_ref[rows, :]).astype(F32)
        outs = []
        for g in groups:
            yg = y[:, g * gw:(g + 1) * gw]
            msq = jnp.mean(yg * yg, axis=-1, keepdims=True)
            outs.append(yg * lax.rsqrt(msq + EPS))
        y_ref[rows, :] = (jnp.concatenate(outs, axis=1) * nw_ref[...]).astype(BF16)

    pending = front(0)
    for ci in range(chunks):
        ready = pending
        if ci + 1 < chunks:
            pending = front(ci + 1)
        back(ci, *ready)


def _ssd(xs, bc, dt, z, dtb, alog, dskip_x, nw, tril, eye, ehead, ecol, batch, seq):
    L = SSD_CHUNK
    ns = seq // SSD_ROWS
    chunks = SSD_ROWS // L
    blk = lambda n: pl.BlockSpec((SSD_ROWS, n), lambda b, s: (b * ns + s, 0))
    return pl.pallas_call(
        functools.partial(_ssd_body, chunks=chunks), grid=(batch, ns),
        in_specs=[blk(SSD_WIDTH), blk(BC_WIDTH), blk(LANES), blk(SSD_WIDTH),
                  _const_spec((1, LANES)), _const_spec((1, LANES)),
                  _const_spec((1, SSD_WIDTH)), _const_spec((1, SSD_WIDTH)),
                  _const_spec((L, L)), _const_spec((L, L)), _const_spec((LANES, SSD_WIDTH)),
                  _const_spec((LANES + L, SSD_HEADS * L))],
        out_specs=blk(SSD_WIDTH),
        out_shape=jax.ShapeDtypeStruct((batch * seq, SSD_WIDTH), BF16),
        scratch_shapes=[pltpu.VMEM((SSD_STATE, SSD_WIDTH), F32),
                        pltpu.VMEM((chunks, L, LANES), F32)],
        compiler_params=_params(2), name="ssd")(
            xs, bc, dt, z, dtb, alog, dskip_x, nw, tril, eye, ehead, ecol)


def _attn_units(units, stat_ref, st_ref):
    blk = BAND_BLOCK
    lane = lax.broadcasted_iota(jnp.int32, (blk, LANES), 1)
    first_head = lane < ATT_HEAD_DIM
    srow = lax.broadcasted_iota(jnp.int32, (LANES, blk), 0)
    first_rows = srow < ATT_HEAD_DIM
    stat_ref[...] = jnp.zeros_like(stat_ref)

    def scores(unit):
        q2 = unit["q"]()
        zero = jnp.zeros_like(q2)
        qst = jnp.concatenate([jnp.where(first_head, q2, zero),
                               jnp.where(first_head, zero, q2)], axis=0)
        return _dot_nt(unit["k"](), qst)

    n_slots = st_ref.shape[0]
    lookahead = n_slots - 1

    def issue(idx):
        st_ref[idx % n_slots] = scores(units[idx])

    for idx in range(min(lookahead, len(units))):
        issue(idx)
    for idx, unit in enumerate(units):
        if idx + lookahead < len(units):
            issue(idx + lookahead)
        bias = unit["bias"]()
        ps, mxs = [], []
        for hh in range(2):
            s = st_ref[idx % n_slots, :, hh * blk:(hh + 1) * blk] + bias
            mx = jnp.max(s, axis=0, keepdims=True)
            ps.append(jnp.exp2(s - mx).astype(BF16))
            mxs.append(mx)
        vt = unit["vt"]()
        ones = jnp.ones((DEN_ROWS, vt.shape[1]), BF16)
        ot = _dot(jnp.concatenate([vt, ones], axis=0), jnp.concatenate(ps, axis=1))
        outs = []
        for hh in range(2):
            den = ot[LANES:LANES + 1, hh * blk:(hh + 1) * blk]
            outs.append(ot[0:LANES, hh * blk:(hh + 1) * blk] * (1.0 / den))
            lse = (mxs[hh] + jnp.log2(den)) * LN2
            for rep in range(HEAD_REP):
                row = rep * ATT_HEADS + 2 * unit["pair"] + hh
                stat_ref[unit["slot"], row:row + 1, :] = lse
        o_t = jnp.where(first_rows, outs[0], outs[1])
        unit["write"](o_t.T.astype(BF16))


def _attn_d1_body(q_ref, kc_ref, kp_ref, vc_ref, vp_ref, bias_ref, o_ref, lse_ref, stat_ref,
                  st_ref):
    blk = BAND_BLOCK
    n_sub = ATT_ROWS // blk
    has_prev = jnp.where(pl.program_id(1) > 0, 1, 0)
    units = []
    for p in range(N_PAIRS):
        sl = slice(p * LANES, (p + 1) * LANES)
        vts = {}

        def vt_block(m, sl=sl, vts=vts):
            if m not in vts:
                v = vp_ref[:, sl] if m == 0 else vc_ref[(m - 1) * blk:m * blk, sl]
                vts[m] = v.T
            return vts[m]

        for j in range(n_sub):
            def write(o, j=j, sl=sl):
                o_ref[j * blk:(j + 1) * blk, sl] = o

            if j == 0:
                k_of = lambda sl=sl: jnp.concatenate([kp_ref[:, sl], kc_ref[0:blk, sl]], axis=0)
                bias_of = lambda: bias_ref[has_prev]
            else:
                k_of = lambda j=j, sl=sl: kc_ref[(j - 1) * blk:(j + 1) * blk, sl]
                bias_of = lambda: bias_ref[1]
            units.append(dict(
                q=lambda j=j, sl=sl: q_ref[j * blk:(j + 1) * blk, sl], k=k_of,
                vt=lambda j=j, f=vt_block: jnp.concatenate([f(j), f(j + 1)], axis=1),
                bias=bias_of, slot=j, pair=p, write=write))
    _attn_units(units, stat_ref, st_ref)
    for j in range(n_sub):
        lse_ref[j * blk:(j + 1) * blk, :] = stat_ref[j].T


def _attn_d4_body(q_ref, kc_ref, kp_ref, vc_ref, vp_ref, bias_ref, o_ref, lse_ref, stat_ref,
                  st_ref):
    blk = BAND_BLOCK
    ni = blk // 4
    n_groups = q_ref.shape[1] // ni
    has_prev = jnp.where(pl.program_id(1) > 0, 1, 0)
    rows = lambda g: slice(g * ni, (g + 1) * ni)
    gather = lambda ref, r, g, sl: [ref[4 * a + r, rows(g), sl] for a in range(4)]

    def window(cur_ref, prev_ref, r, g, sl):
        before = gather(prev_ref, r, 0, sl) if g == 0 else gather(cur_ref, r, g - 1, sl)
        return jnp.concatenate(before, axis=0), jnp.concatenate(gather(cur_ref, r, g, sl), axis=0)

    units = []
    for p in range(N_PAIRS):
        sl = slice(p * LANES, (p + 1) * LANES)
        for g in range(n_groups):
            for r in range(4):
                def write(o, r=r, g=g, sl=sl):
                    for a in range(4):
                        o_ref[4 * a + r, rows(g), sl] = o[a * ni:(a + 1) * ni, :]

                units.append(dict(
                    q=lambda r=r, g=g, sl=sl: jnp.concatenate(gather(q_ref, r, g, sl), axis=0),
                    k=lambda r=r, g=g, sl=sl: jnp.concatenate(
                        window(kc_ref, kp_ref, r, g, sl), axis=0),
                    vt=lambda r=r, g=g, sl=sl: jnp.concatenate(
                        [v.T for v in window(vc_ref, vp_ref, r, g, sl)], axis=1),
                    bias=(lambda: bias_ref[has_prev]) if g == 0 else (lambda: bias_ref[1]),
                    slot=g * 4 + r, pair=p, write=write))
    _attn_units(units, stat_ref, st_ref)
    for g in range(n_groups):
        for r in range(4):
            lt = stat_ref[g * 4 + r].T
            for a in range(4):
                lse_ref[4 * a + r, rows(g), :] = lt[a * ni:(a + 1) * ni, :]


def _attn_d16_body(q_ref, k_ref, v_ref, bias_ref, o_ref, lse_ref, stat_ref, st_ref):
    n_sub = q_ref.shape[0]
    units = []
    for p in range(N_PAIRS):
        sl = slice(p * LANES, (p + 1) * LANES)
        for j in range(n_sub):
            def write(o, j=j, sl=sl):
                o_ref[j, :, sl] = o

            units.append(dict(
                q=lambda j=j, sl=sl: q_ref[j, :, sl], k=lambda j=j, sl=sl: k_ref[j, :, sl],
                vt=lambda j=j, sl=sl: v_ref[j, :, sl].T,
                bias=lambda: bias_ref[0], slot=j, pair=p, write=write))
    _attn_units(units, stat_ref, st_ref)
    for j in range(n_sub):
        lse_ref[j] = stat_ref[j].T


def _attn_scratch(n_sub, nk, lookahead=SCORE_LOOKAHEAD):
    return [pltpu.VMEM((n_sub, LANES, BAND_BLOCK), F32),
            pltpu.VMEM((lookahead + 1, nk, 2 * BAND_BLOCK), F32)]


def _band_bias(kk, qi, need_prev):
    ok = (kk[:, None] >= qi[None, :]) & (kk[:, None] <= qi[None, :] + BAND_BLOCK)
    if need_prev is not None:
        ok = ok & need_prev[:, None]
    return np.where(ok, 0.0, -np.inf).astype(np.float32)


def _attn_d1(q, k, v, batch, seq):
    blk = BAND_BLOCK
    w = ATT_WIDTH
    nt = seq // ATT_ROWS
    per = ATT_ROWS // blk
    kk = np.arange(2 * blk)
    qi = np.arange(blk)
    bias = jnp.asarray(np.stack([_band_bias(kk, qi, kk >= blk), _band_bias(kk, qi, None)]))
    cur = lambda n: pl.BlockSpec((ATT_ROWS, n), lambda b, i: (b * nt + i, 0))
    prev = pl.BlockSpec((blk, w), lambda b, i: (b * nt * per + jnp.maximum(i * per - 1, 0), 0))
    return pl.pallas_call(
        _attn_d1_body, grid=(batch, nt),
        in_specs=[cur(w), cur(w), prev, cur(w), prev, _const_spec(bias.shape)],
        out_specs=[cur(w), cur(LANES)],
        out_shape=[jax.ShapeDtypeStruct((batch * seq, w), BF16),
                   jax.ShapeDtypeStruct((batch * seq, LANES), F32)],
        scratch_shapes=_attn_scratch(per, 2 * blk, SCORE_LOOKAHEAD_D1),
        compiler_params=_params(2), name="attn_d1")(q, k, k, v, v, bias)


def _attn_d4(q16, k16, v16, batch, seq):
    blk = BAND_BLOCK
    w = ATT_WIDTH
    ni = blk // 4
    nb = seq // 4 // blk
    pos = np.arange(blk)
    true_i = 4 * (pos % ni) + pos // ni
    kk = np.concatenate([true_i, true_i + blk])
    bias = jnp.asarray(np.stack([_band_bias(kk, true_i, kk >= blk), _band_bias(kk, true_i, None)]))
    per = ATT_ROWS // (4 * blk)
    cur = lambda n: pl.BlockSpec((None, RES, per * ni, n), lambda b, i: (b, 0, i, 0))
    prev = pl.BlockSpec((None, RES, ni, w), lambda b, i: (b, 0, jnp.maximum(i * per - 1, 0), 0))
    return pl.pallas_call(
        _attn_d4_body, grid=(batch, nb // per),
        in_specs=[cur(w), cur(w), prev, cur(w), prev, _const_spec(bias.shape)],
        out_specs=[cur(w), cur(LANES)],
        out_shape=[jax.ShapeDtypeStruct((batch, RES, seq // RES, w), BF16),
                   jax.ShapeDtypeStruct((batch, RES, seq // RES, LANES), F32)],
        scratch_shapes=_attn_scratch(4 * per, 2 * blk),
        compiler_params=_params(2), name="attn_d4")(q16, k16, k16, v16, v16, bias)


def _attn_d16(q16, k16, v16, batch, seq):
    blk = BAND_BLOCK
    w = ATT_WIDTH
    n_sub = ATT_ROWS // blk
    assert seq // RES == blk and RES % n_sub == 0
    kk = np.arange(blk)
    bias = jnp.asarray(_band_bias(kk + blk, kk, None)[None])
    spec = lambda n: pl.BlockSpec((None, n_sub, blk, n), lambda b, g: (b, g, 0, 0))
    return pl.pallas_call(
        _attn_d16_body, grid=(batch, RES // n_sub),
        in_specs=[spec(w), spec(w), spec(w), _const_spec(bias.shape)],
        out_specs=[spec(w), spec(LANES)],
        out_shape=[jax.ShapeDtypeStruct((batch, RES, blk, w), BF16),
                   jax.ShapeDtypeStruct((batch, RES, blk, LANES), F32)],
        scratch_shapes=_attn_scratch(n_sub, blk),
        compiler_params=_params(2), name="attn_d16")(q16, k16, v16, bias)


def _outproj_body(x_ref, ys_ref, o1_ref, l1_ref, o4_ref, l4_ref, o16_ref, l16_ref, ehead_ref,
                  w_ref, out_ref, small_ref, small2_ref, perm_ref, perm2_ref):
    tm = x_ref.shape[0]
    ni = tm // RES

    nq = tm // 4

    def to_natural(l_ref):
        for r in range(4):
            for a in range(4):
                small2_ref[pl.ds(r * nq + a, ni, stride=4), :] = l_ref[4 * a + r]
        for r in range(4):
            small_ref[pl.ds(r, nq, stride=4), :] = small2_ref[r * nq:(r + 1) * nq, :]
        return small_ref[...]

    def to_l16(val):
        small_ref[...] = val
        for r in range(4):
            small2_ref[r * nq:(r + 1) * nq, :] = small_ref[pl.ds(r, nq, stride=4), :]
        parts = {4 * a + r: small2_ref[pl.ds(r * nq + a, ni, stride=4), :]
                 for r in range(4) for a in range(4)}
        return jnp.concatenate([parts[c] for c in range(RES)], axis=0)

    l1 = l1_ref[...]
    l4 = to_natural(l4_ref)
    l16 = to_natural(l16_ref)
    m = jnp.maximum(jnp.maximum(l1, l4), l16)
    e1, e4, e16 = jnp.exp(l1 - m), jnp.exp(l4 - m), jnp.exp(l16 - m)
    inv = 1.0 / (e1 + e4 + e16)
    a1 = _split3_lanes(e1 * inv)
    a4 = _split3_lanes(to_l16(e4 * inv))
    a16 = _split3_lanes(to_l16(e16 * inv))
    acc = x_ref[...] + _dot(ys_ref[...], w_ref[0:SSD_WIDTH, :])
    per = MXU_COLS // LANES
    for c in range(ATT_WIDTH // MXU_COLS):
        cols = slice(c * MXU_COLS, (c + 1) * MXU_COLS)
        ehead = ehead_ref[:, cols]
        o4 = o4_ref[:, :, cols].reshape(tm, MXU_COLS).astype(F32)
        o16 = o16_ref[:, :, cols].reshape(tm, MXU_COLS).astype(F32)
        y_l16 = _dot(a4, ehead) * o4 + _dot(a16, ehead) * o16
        nj = tm // 4
        for s in range(per):
            slab = c * per + s
            for r in range(4):
                for a in range(4):
                    perm2_ref[slab, pl.ds(r * nj + a, ni, stride=4), :] = (
                        y_l16[(4 * a + r) * ni:(4 * a + r + 1) * ni, s * LANES:(s + 1) * LANES])
            for r in range(4):
                perm_ref[slab, pl.ds(r, nj, stride=4), :] = perm2_ref[slab, r * nj:(r + 1) * nj, :]
        y_nat = jnp.concatenate([perm_ref[c * per + s] for s in range(per)], axis=1)
        y_att = _dot(a1, ehead) * o1_ref[:, cols].astype(F32) + y_nat
        acc = acc + _dot(y_att.astype(BF16),
                         w_ref[SSD_WIDTH + c * MXU_COLS:SSD_WIDTH + (c + 1) * MXU_COLS, :])
    out_ref[...] = acc


def _outproj(x, y_ssd, o1, l1, o4, l4, o16, l16, ehead, w, seq):
    t, d = x.shape
    tps = seq // TM
    ni = TM // RES
    row = lambda n: pl.BlockSpec((TM, n), lambda i: (i, 0))
    slab = lambda n: pl.BlockSpec((None, RES, ni, n), lambda i: (i // tps, 0, i % tps, 0))
    return pl.pallas_call(
        _outproj_body, grid=(t // TM,),
        in_specs=[row(d), row(SSD_WIDTH), row(ATT_WIDTH), row(LANES), slab(ATT_WIDTH), slab(LANES),
                  slab(ATT_WIDTH), slab(LANES), _const_spec((LANES, ATT_WIDTH)),
                  _const_spec((SSD_WIDTH + ATT_WIDTH, d))],
        out_specs=row(d), out_shape=jax.ShapeDtypeStruct((t, d), F32),
        scratch_shapes=[pltpu.VMEM((TM, LANES), F32), pltpu.VMEM((TM, LANES), F32),
                        pltpu.VMEM((ATT_WIDTH // LANES, TM, LANES), F32),
                        pltpu.VMEM((ATT_WIDTH // LANES, TM, LANES), F32)],
        compiler_params=_params(1), name="outproj")(
            x, y_ssd, o1, l1, o4, l4, o16, l16, ehead, w)


def _head_expand(width_per_head):
    r = jnp.arange(LANES)[:, None]
    c = jnp.arange(SSD_HEADS * width_per_head)[None, :]
    return ((r < HEAD_REP * SSD_HEADS) & (r % SSD_HEADS == c // width_per_head)).astype(BF16)


def _rep_heads(v):
    return jnp.pad(jnp.tile(v.astype(F32), HEAD_REP), (0, LANES - HEAD_REP * SSD_HEADS))[None, :]


def kernel(x, ffn1_norm, ffn1_w_gate, ffn1_w_up, ffn1_w_down, mix_norm, w_in, conv_w, conv_b,
           dt_bias, a_log, d_skip, ssd_norm, q_norm, k_norm, w_out, ffn2_norm, ffn2_w_gate,
           ffn2_w_up, ffn2_w_down):
    batch, seq, d = x.shape
    depth = w_in.shape[0]
    assert d == D_MODEL and seq == RES * BAND_BLOCK
    assert all(wd // dl == BAND_BLOCK for wd, dl in ATT_BRANCHES)
    assert seq % TM == 0 and seq % ATT_ROWS == 0 and seq % SSD_ROWS == 0
    t = batch * seq

    tril = jnp.tril(jnp.ones((SSD_CHUNK, SSD_CHUNK), BF16))
    eye = jnp.eye(SSD_CHUNK, dtype=BF16)
    ehead = _head_expand(SSD_HEAD_DIM)
    masked = jnp.where(tril > 0, 0.0, MASKED).astype(BF16)
    ecol = jnp.concatenate([_head_expand(SSD_CHUNK), jnp.tile(masked, (1, SSD_HEADS))], axis=0)
    half = jnp.arange(PROJ_COLS) // ATT_HEAD_DIM
    bd2 = ((half[:, None] == half[None, :]).astype(F32) / ATT_HEAD_DIM).astype(BF16)
    scale = LOG2E / math.sqrt(ATT_HEAD_DIM)

    xf = x.reshape(t, d)
    for i in range(depth):
        dt_cols = jnp.pad(jnp.tile(w_in[i][:, COL_DT:COL_DT + SSD_HEADS], (1, HEAD_REP)),
                          ((0, 0), (0, LANES - HEAD_REP * SSD_HEADS)))
        w_ssd = jnp.concatenate([w_in[i][:, :COL_DT], dt_cols], axis=1).astype(BF16)
        w_att = w_in[i][:, COL_DT + SSD_HEADS:].astype(BF16)
        qg = jnp.tile(q_norm[i].astype(F32) * scale, LANES // ATT_HEAD_DIM)[None, :]
        kg = jnp.tile(k_norm[i].astype(F32), LANES // ATT_HEAD_DIM)[None, :]

        xf = _ffn(xf, ffn1_norm[i][None, :], ffn1_w_gate[i].astype(BF16),
                  ffn1_w_up[i].astype(BF16), ffn1_w_down[i].astype(BF16))
        z, xs, bc, dt = _proj_ssd(xf, mix_norm[i][None, :], w_ssd, 0.5 * conv_w[i],
                                  0.5 * conv_b[i][None, :], seq)
        qn, kn, vn, q16, k16, v16 = _proj_att(xf, mix_norm[i][None, :], w_att, qg, kg, bd2,
                                              batch, seq)
        y_ssd = _ssd(xs, bc, dt, z, _rep_heads(dt_bias[i]), _rep_heads(a_log[i]),
                     jnp.repeat(d_skip[i].astype(F32), SSD_HEAD_DIM)[None, :],
                     ssd_norm[i][None, :], tril, eye, ehead, ecol, batch, seq)
        o1, l1 = _attn_d1(qn, kn, vn, batch, seq)
        o4, l4 = _attn_d4(q16, k16, v16, batch, seq)
        o16, l16 = _attn_d16(q16, k16, v16, batch, seq)
        xf = _outproj(xf, y_ssd, o1, l1, o4, l4, o16, l16, ehead, w_out[i].astype(BF16), seq)
        xf = _ffn(xf, ffn2_norm[i][None, :], ffn2_w_gate[i].astype(BF16),
                  ffn2_w_up[i].astype(BF16), ffn2_w_down[i].astype(BF16))
    return xf.reshape(batch, seq, d)
```
